```python
import jax, jax.numpy as jnp
from jax import lax
import numpy as np

D_MODEL = 1024
BATCH = 4
SEQ = 8192
DEPTH = 2

ATT_HEADS = 16
ATT_KV_HEADS = 2
ATT_HEAD_DIM = 64
WINDOW = 128
ATT_BLOCK = 128
SSD_EXPAND = 2
SSD_D_INNER = SSD_EXPAND * D_MODEL
SSD_HEAD_DIM = 64
SSD_HEADS = SSD_D_INNER // SSD_HEAD_DIM
SSD_GROUPS = 4
SSD_STATE = 128
SSD_CONV = 4
SSD_CHUNK = 128
FFN_HIDDEN = -(-8 * D_MODEL // (3 * 256)) * 256

LN_EPS = 1e-5
RMS_EPS = 1e-5
DEEPNORM_ALPHA = (2 * DEPTH) ** 0.25
DEEPNORM_BETA = (8 * DEPTH) ** -0.25

Q_DIM = ATT_HEADS * ATT_HEAD_DIM
KV_DIM = ATT_KV_HEADS * ATT_HEAD_DIM
BC_DIM = SSD_GROUPS * SSD_STATE
CONV_DIM = SSD_D_INNER + 2 * BC_DIM
IN_SIZES = (Q_DIM, KV_DIM, KV_DIM, SSD_D_INNER, SSD_D_INNER, BC_DIM, BC_DIM, SSD_HEADS, 2 * D_MODEL)
IN_DIM = sum(IN_SIZES)

kernel_name = 'hybrid_ssd_swa_sink_alibi_deepnorm'


def _split(t, sizes):
    offs = np.cumsum(sizes)[:-1].tolist()
    return jnp.split(t, offs, axis=-1)


def layer_norm(x, g, b):
    xf = x.astype(jnp.float32)
    mu = jnp.mean(xf, axis=-1, keepdims=True)
    var = jnp.mean(jnp.square(xf - mu), axis=-1, keepdims=True)
    return ((xf - mu) * lax.rsqrt(var + LN_EPS) * g + b).astype(x.dtype)


def grouped_rms_norm(y, w):
    yg = y.reshape(*y.shape[:-1], SSD_GROUPS, -1)
    yg = yg * lax.rsqrt(jnp.mean(jnp.square(yg), axis=-1, keepdims=True) + RMS_EPS)
    return yg.reshape(y.shape) * w


def causal_depthwise_conv(u, w, b):
    c = u.shape[-1]
    out = lax.conv_general_dilated(
        u, w[:, None, :].astype(u.dtype), window_strides=(1,),
        padding=[(SSD_CONV - 1, 0)], dimension_numbers=('NWC', 'WIO', 'NWC'),
        feature_group_count=c)
    return out + b


def segsum(a):
    t = a.shape[-1]
    cs = jnp.cumsum(a, axis=-1)
    diff = cs[..., :, None] - cs[..., None, :]
    mask = jnp.tril(jnp.ones((t, t), dtype=bool))
    return jnp.where(mask, diff, -jnp.inf)


def ssd_chunked_scan(xh, dt, a, b_ssm, c_ssm):
    bsz, seqlen, nh, hp = xh.shape
    ng, ns = b_ssm.shape[2], b_ssm.shape[3]
    ne = nh // ng
    nc = seqlen // SSD_CHUNK
    T = SSD_CHUNK
    X = (xh.astype(jnp.float32) * dt[..., None]).reshape(bsz, nc, T, ng, ne, hp)
    dA = jnp.moveaxis((dt * a).reshape(bsz, nc, T, ng, ne), 2, -1)
    a_cum = jnp.cumsum(dA, axis=-1)
    Bc = b_ssm.astype(jnp.float32).reshape(bsz, nc, T, ng, ns)
    Cc = c_ssm.astype(jnp.float32).reshape(bsz, nc, T, ng, ns)
    Lmat = jnp.exp(segsum(dA))
    CB = jnp.einsum('bclgn,bcsgn->bcgls', Cc, Bc)
    y_diag = jnp.einsum('bcgels,bcsgep->bclgep', Lmat * CB[:, :, :, None], X)
    decay_states = jnp.exp(a_cum[..., -1:] - a_cum)
    states = jnp.einsum('bclgn,bcgel,bclgep->bcgepn', Bc, decay_states, X)
    chunk_decay = jnp.exp(a_cum[..., -1])

    def step(h, inp):
        s_c, d_c = inp
        return h * d_c[..., None, None] + s_c, h

    h0 = jnp.zeros((bsz, ng, ne, hp, ns), jnp.float32)
    _, h_in = lax.scan(step, h0, (jnp.moveaxis(states, 1, 0), jnp.moveaxis(chunk_decay, 1, 0)))
    h_in = jnp.moveaxis(h_in, 0, 1)
    y_off = jnp.einsum('bclgn,bcgepn,bcgel->bclgep', Cc, h_in, jnp.exp(a_cum))
    return (y_diag + y_off).reshape(bsz, seqlen, nh, hp)


def alibi_slopes(n_heads):
    return jnp.exp2(-8.0 * jnp.arange(1, n_heads + 1, dtype=jnp.float32) / n_heads)


def sliding_window_sink_attention(q, k, v, sinks):
    bsz, seqlen, nh, hd = q.shape
    nkv = k.shape[2]
    ng = nh // nkv
    nb = seqlen // ATT_BLOCK
    qb = (q * (hd ** -0.5)).reshape(bsz, nb, ATT_BLOCK, nkv, ng, hd)
    pad = ((0, 0), (ATT_BLOCK, 0), (0, 0), (0, 0))
    kb = jnp.pad(k, pad).reshape(bsz, nb + 1, ATT_BLOCK, nkv, hd)
    vb = jnp.pad(v, pad).reshape(bsz, nb + 1, ATT_BLOCK, nkv, hd)
    k_band = jnp.concatenate([kb[:, :-1], kb[:, 1:]], axis=2)
    v_band = jnp.concatenate([vb[:, :-1], vb[:, 1:]], axis=2)
    scores = jnp.einsum('bnqkgd,bnskd->bnkgqs', qb, k_band).astype(jnp.float32)
    qi = jnp.arange(ATT_BLOCK)
    kj = jnp.arange(2 * ATT_BLOCK)
    rel = qi[:, None] + ATT_BLOCK - kj[None, :]
    key_pos = jnp.arange(nb)[:, None] * ATT_BLOCK - ATT_BLOCK + kj[None, :]
    valid = (rel >= 0)[None] & (rel < WINDOW)[None] & (key_pos >= 0)[:, None, :]
    slopes = alibi_slopes(nh).reshape(nkv, ng)
    bias = -slopes[:, :, None, None] * rel.astype(jnp.float32)
    scores = jnp.where(valid[None, :, None, None], scores + bias, -jnp.inf)
    sink = sinks.astype(jnp.float32).reshape(1, 1, nkv, ng, 1, 1)
    m = jnp.maximum(jnp.max(scores, axis=-1, keepdims=True), sink)
    p = jnp.exp(scores - m)
    p = p / (jnp.sum(p, axis=-1, keepdims=True) + jnp.exp(sink - m))
    out = jnp.einsum('bnkgqs,bnskd->bnqkgd', p.astype(v.dtype), v_band)
    return out.reshape(bsz, seqlen, nh * hd)


def token_mixer(h, w_in, conv_w, conv_b, dt_bias, a_log, d_skip, ssd_norm_w, att_sinks,
                w_ssd_out, w_att_out, w_mix_out):
    bsz, seqlen, _ = h.shape
    proj = h @ w_in
    q, k, v, z, xs, b_ssm, c_ssm, dt_raw, gate_logits = _split(proj, IN_SIZES)
    xbc = jax.nn.silu(causal_depthwise_conv(jnp.concatenate([xs, b_ssm, c_ssm], axis=-1), conv_w, conv_b))
    xs, b_ssm, c_ssm = _split(xbc, (SSD_D_INNER, BC_DIM, BC_DIM))
    xh = xs.reshape(bsz, seqlen, SSD_HEADS, SSD_HEAD_DIM)
    dt = jax.nn.softplus(dt_raw.astype(jnp.float32) + dt_bias)
    a = -jnp.exp(a_log.astype(jnp.float32))
    y = ssd_chunked_scan(xh, dt, a,
                         b_ssm.reshape(bsz, seqlen, SSD_GROUPS, SSD_STATE),
                         c_ssm.reshape(bsz, seqlen, SSD_GROUPS, SSD_STATE))
    y = y + d_skip[:, None] * xh
    y = y.reshape(bsz, seqlen, SSD_D_INNER) * jax.nn.silu(z.astype(jnp.float32))
    y_a = grouped_rms_norm(y, ssd_norm_w).astype(h.dtype) @ w_ssd_out
    att = sliding_window_sink_attention(
        q.reshape(bsz, seqlen, ATT_HEADS, ATT_HEAD_DIM),
        k.reshape(bsz, seqlen, ATT_KV_HEADS, ATT_HEAD_DIM),
        v.reshape(bsz, seqlen, ATT_KV_HEADS, ATT_HEAD_DIM), att_sinks)
    y_b = att @ w_att_out
    g_a, g_b = jnp.split(jax.nn.sigmoid(gate_logits), 2, axis=-1)
    return (g_a * y_a + g_b * y_b) @ w_mix_out


def swiglu_ffn(h, w_gate, w_up, w_down):
    return (jax.nn.silu(h @ w_gate) * (h @ w_up)) @ w_down


def setup_inputs(seed: int = 0) -> dict:
    key = jax.random.key(seed)
    ks = jax.random.split(key, 24)
    f32 = jnp.float32

    def nrm(k, shape, scale):
        return jax.random.normal(k, shape, f32) * scale

    x = nrm(ks[0], (BATCH, SEQ, D_MODEL), 1.0)
    ln_in_g = 1.0 + nrm(ks[1], (D_MODEL,), 0.02)
    ln_in_b = nrm(ks[2], (D_MODEL,), 0.02)
    col_scale = jnp.concatenate([
        jnp.ones((Q_DIM + KV_DIM,), f32), jnp.full((KV_DIM,), DEEPNORM_BETA, f32),
        jnp.ones((SSD_D_INNER,), f32), jnp.full((SSD_D_INNER,), DEEPNORM_BETA, f32),
        jnp.ones((2 * BC_DIM + SSD_HEADS + 2 * D_MODEL,), f32)])
    w_in = nrm(ks[3], (DEPTH, D_MODEL, IN_DIM), D_MODEL ** -0.5) * col_scale
    conv_w = nrm(ks[4], (DEPTH, SSD_CONV, CONV_DIM), SSD_CONV ** -0.5)
    conv_b = nrm(ks[5], (DEPTH, CONV_DIM), 0.01)
    dt0 = jnp.exp(jax.random.uniform(ks[6], (DEPTH, SSD_HEADS), f32)
                  * (jnp.log(0.1) - jnp.log(0.001)) + jnp.log(0.001))
    dt_bias = dt0 + jnp.log(-jnp.expm1(-dt0))
    a_log = jnp.log(jax.random.uniform(ks[7], (DEPTH, SSD_HEADS), f32, 1.0, 16.0))
    d_skip = 1.0 + nrm(ks[8], (DEPTH, SSD_HEADS), 0.1)
    ssd_norm_w = 1.0 + nrm(ks[9], (DEPTH, SSD_D_INNER), 0.02)
    att_sinks = nrm(ks[10], (DEPTH, ATT_HEADS), 0.5)
    w_ssd_out = nrm(ks[11], (DEPTH, SSD_D_INNER, D_MODEL), SSD_D_INNER ** -0.5 * DEEPNORM_BETA)
    w_att_out = nrm(ks[12], (DEPTH, Q_DIM, D_MODEL), Q_DIM ** -0.5 * DEEPNORM_BETA)
    w_mix_out = nrm(ks[13], (DEPTH, D_MODEL, D_MODEL), D_MODEL ** -0.5 * DEEPNORM_BETA)
    ln_mix_g = 1.0 + nrm(ks[14], (DEPTH, D_MODEL), 0.02)
    ln_mix_b = nrm(ks[15], (DEPTH, D_MODEL), 0.02)
    w_ffn_gate = nrm(ks[16], (DEPTH, D_MODEL, FFN_HIDDEN), D_MODEL ** -0.5 * DEEPNORM_BETA)
    w_ffn_up = nrm(ks[17], (DEPTH, D_MODEL, FFN_HIDDEN), D_MODEL ** -0.5 * DEEPNORM_BETA)
    w_ffn_down = nrm(ks[18], (DEPTH, FFN_HIDDEN, D_MODEL), FFN_HIDDEN ** -0.5 * DEEPNORM_BETA)
    ln_ffn_g = 1.0 + nrm(ks[19], (DEPTH, D_MODEL), 0.02)
    ln_ffn_b = nrm(ks[20], (DEPTH, D_MODEL), 0.02)
    return {'x': x, 'ln_in_g': ln_in_g, 'ln_in_b': ln_in_b, 'w_in': w_in,
            'conv_w': conv_w, 'conv_b': conv_b, 'dt_bias': dt_bias, 'a_log': a_log,
            'd_skip': d_skip, 'ssd_norm_w': ssd_norm_w, 'att_sinks': att_sinks,
            'w_ssd_out': w_ssd_out, 'w_att_out': w_att_out, 'w_mix_out': w_mix_out,
            'ln_mix_g': ln_mix_g, 'ln_mix_b': ln_mix_b, 'w_ffn_gate': w_ffn_gate,
            'w_ffn_up': w_ffn_up, 'w_ffn_down': w_ffn_down,
            'ln_ffn_g': ln_ffn_g, 'ln_ffn_b': ln_ffn_b}


def reference(x, ln_in_g, ln_in_b, w_in, conv_w, conv_b, dt_bias, a_log, d_skip, ssd_norm_w,
              att_sinks, w_ssd_out, w_att_out, w_mix_out, ln_mix_g, ln_mix_b,
              w_ffn_gate, w_ffn_up, w_ffn_down, ln_ffn_g, ln_ffn_b):
    h = layer_norm(x, ln_in_g, ln_in_b)
    for l in range(DEPTH):
        mix = token_mixer(h, w_in[l], conv_w[l], conv_b[l], dt_bias[l], a_log[l], d_skip[l],
                          ssd_norm_w[l], att_sinks[l], w_ssd_out[l], w_att_out[l], w_mix_out[l])
        h = layer_norm(DEEPNORM_ALPHA * h + mix, ln_mix_g[l], ln_mix_b[l])
        ffn = swiglu_ffn(h, w_ffn_gate[l], w_ffn_up[l], w_ffn_down[l])
        h = layer_norm(DEEPNORM_ALPHA * h + ffn, ln_ffn_g[l], ln_ffn_b[l])
    return h
```

```python
import functools

import jax
import jax.numpy as jnp
from jax import lax
from jax.experimental import pallas as pl
from jax.experimental.pallas import tpu as pltpu

D_MODEL = 1024
ATT_HEADS = 16
ATT_KV_HEADS = 2
ATT_HEAD_DIM = 64
ATT_BLOCK = 128
SSD_D_INNER = 2048
SSD_HEAD_DIM = 64
SSD_HEADS = 32
SSD_GROUPS = 4
SSD_HEADS_PER_GROUP = SSD_HEADS // SSD_GROUPS
SSD_STATE = 128
SSD_CONV = 4
SSD_CHUNK = 128
FFN_HIDDEN = 2816
DEPTH = 2
LN_EPS = 1e-5
RMS_EPS = 1e-5
DEEPNORM_ALPHA = (2 * DEPTH) ** 0.25

Q_DIM = ATT_HEADS * ATT_HEAD_DIM
KV_DIM = ATT_KV_HEADS * ATT_HEAD_DIM
BC_DIM = SSD_GROUPS * SSD_STATE
CONV_DIM = SSD_D_INNER + 2 * BC_DIM

OFF_Q = 0
OFF_KV = OFF_Q + Q_DIM
OFF_Z = OFF_KV + 2 * KV_DIM
OFF_XBC = OFF_Z + SSD_D_INNER
OFF_DT = OFF_XBC + CONV_DIM
LANES = 128
SUBLANES = 8
FRONT_DIM = OFF_DT + LANES
OFF_GATES = OFF_DT + SSD_HEADS

SEQ_TILE = 128
ROW_TILE = 512
VMEM_LIMIT = 56 * 1024 * 1024

_F32 = jnp.float32
_BF16 = jnp.bfloat16
_NT = (((1,), (1,)), ((), ()))


def _dot(a, b):
    return jnp.dot(a, b, preferred_element_type=_F32)


def _sigmoid(x):
    return 1.0 / (1.0 + jnp.exp(-x))


def _layer_norm(x, g, b):
    mu = jnp.mean(x, axis=-1, keepdims=True)
    xc = x - mu
    var = jnp.mean(xc * xc, axis=-1, keepdims=True)
    return xc * lax.rsqrt(var + LN_EPS) * g + b


def _resident(shape):
    nd = len(shape)
    return pl.BlockSpec(shape, lambda *_: (0,) * nd, pipeline_mode=pl.Buffered(1))


def _ln_kernel(x_ref, g_ref, b_ref, o_ref):
    o_ref[...] = _layer_norm(x_ref[...], g_ref[...], b_ref[...])


def _input_layer_norm(x2d, g, b):
    n = x2d.shape[0]
    return pl.pallas_call(
        _ln_kernel,
        grid=(n // ROW_TILE,),
        in_specs=[pl.BlockSpec((ROW_TILE, D_MODEL), lambda i: (i, 0)),
                  _resident((1, D_MODEL)), _resident((1, D_MODEL))],
        out_specs=pl.BlockSpec((ROW_TILE, D_MODEL), lambda i: (i, 0)),
        out_shape=jax.ShapeDtypeStruct((n, D_MODEL), _F32),
        compiler_params=pltpu.CompilerParams(dimension_semantics=("arbitrary",)),
        name="input_ln",
    )(x2d, g, b)


def _attention_block(q, kv_prev, kv_cur, sinks_ref, first_block):
    band = jnp.concatenate([kv_prev, kv_cur], axis=0)
    row = lax.broadcasted_iota(jnp.int32, (ATT_BLOCK, 2 * ATT_BLOCK), 0)
    col = lax.broadcasted_iota(jnp.int32, (ATT_BLOCK, 2 * ATT_BLOCK), 1)
    rel = row + ATT_BLOCK - col
    first_key = jnp.where(first_block, ATT_BLOCK, 0)
    valid = (rel >= 0) & (rel < ATT_BLOCK) & (col >= first_key)
    rel_f = rel.astype(_F32)
    group = ATT_HEADS // ATT_KV_HEADS
    outs = []
    for h in range(ATT_HEADS):
        kvh = h // group
        kb = band[:, kvh * ATT_HEAD_DIM:(kvh + 1) * ATT_HEAD_DIM]
        vb = band[:, KV_DIM + kvh * ATT_HEAD_DIM:KV_DIM + (kvh + 1) * ATT_HEAD_DIM]
        qh = q[:, h * ATT_HEAD_DIM:(h + 1) * ATT_HEAD_DIM]
        slope = 2.0 ** (-8.0 * (h + 1) / ATT_HEADS)
        s = lax.dot_general(qh, kb, _NT, preferred_element_type=_F32)
        s = jnp.where(valid, s - slope * rel_f, -jnp.inf)
        sink = sinks_ref[h]
        m = jnp.maximum(jnp.max(s, axis=-1, keepdims=True), sink)
        p = jnp.exp(s - m)
        denom = jnp.sum(p, axis=-1, keepdims=True) + jnp.exp(sink - m)
        outs.append(_dot(p.astype(_BF16), vb) / denom)
    return jnp.concatenate(outs, axis=-1)


def _ssd_chunk(xs, b_mat, c_mat, dt, a_row, s_ref):
    t = SSD_CHUNK
    d_a = dt * a_row
    d_a_hi = d_a.astype(_BF16)
    d_a_lo = (d_a - d_a_hi.astype(_F32)).astype(_BF16)
    r = lax.broadcasted_iota(jnp.int32, (t, t), 0)
    c = lax.broadcasted_iota(jnp.int32, (t, t), 1)
    causal = r >= c
    tri = jnp.where(causal, 1.0, 0.0).astype(_BF16)
    cum = _dot(tri, d_a_hi) + _dot(tri, d_a_lo)
    cum_t = cum.T
    dt_t = dt.T
    total_col = cum_t[:, t - 1:t]
    w_state_t = jnp.exp(total_col - cum_t) * dt_t
    e_cum = jnp.exp(cum)
    chunk_decay = e_cum[t - 1:t, :]
    xs_b = xs.astype(_BF16)
    ys = []
    for g in range(SSD_GROUPS):
        c_g = c_mat[:, g * SSD_STATE:(g + 1) * SSD_STATE]
        b_g = b_mat[:, g * SSD_STATE:(g + 1) * SSD_STATE]
        cb = lax.dot_general(c_g.astype(_BF16), b_g.astype(_BF16), _NT, preferred_element_type=_F32)
        b_g_t = b_g.T
        for e in range(SSD_HEADS_PER_GROUP):
            h = g * SSD_HEADS_PER_GROUP + e
            seg = jnp.where(causal, cum[:, h:h + 1] - cum_t[h:h + 1, :], -jnp.inf)
            m_intra = jnp.exp(seg) * (cb * dt_t[h:h + 1, :])
            c_scaled = c_g * e_cum[:, h:h + 1]
            lhs = jnp.concatenate([m_intra, c_scaled], axis=1).astype(_BF16)
            state = s_ref[h]
            x_h = xs_b[:, h * SSD_HEAD_DIM:(h + 1) * SSD_HEAD_DIM]
            rhs = jnp.concatenate([x_h, state.astype(_BF16)], axis=0)
            ys.append(_dot(lhs, rhs))
            bw_t = (b_g_t * w_state_t[h:h + 1, :]).astype(_BF16)
            s_ref[h] = state * chunk_decay[:, h:h + 1] + _dot(bw_t, x_h)
    return jnp.concatenate(ys, axis=-1)


def _mixer_front_kernel(sinks_ref, h_ref, w_ref, convw_ref, convb_ref, dtb_ref, alog_ref, dskip_ref,
                        normw_ref, y_ref, att_ref, s_ref, kv_ref, xbuf_ref):
    i = pl.program_id(1)
    tm = SEQ_TILE
    halo = SUBLANES

    @pl.when(i == 0)
    def _():
        s_ref[...] = jnp.zeros_like(s_ref)
        kv_ref[...] = jnp.zeros_like(kv_ref)
        xbuf_ref[0:halo, :] = jnp.zeros((halo, CONV_DIM), _F32)

    hb = h_ref[...].astype(_BF16)

    qkv = _dot(hb, w_ref[:, OFF_Q:OFF_Z])
    q = (qkv[:, :Q_DIM] * (ATT_HEAD_DIM ** -0.5)).astype(_BF16)
    kv_cur = qkv[:, Q_DIM:].astype(_BF16)
    att = _attention_block(q, kv_ref[...], kv_cur, sinks_ref, i == 0)
    att_ref[...] = att.astype(_BF16)
    kv_ref[...] = kv_cur

    z = _dot(hb, w_ref[:, OFF_Z:OFF_XBC])
    xbc_pre = _dot(hb, w_ref[:, OFF_XBC:OFF_DT])
    dt_raw = _dot(hb, w_ref[:, OFF_DT:FRONT_DIM])
    xbuf_ref[halo:halo + tm, :] = xbc_pre
    acc = convb_ref[...] + convw_ref[SSD_CONV - 1:SSD_CONV, :] * xbc_pre
    for k in range(SSD_CONV - 1):
        start = halo - (SSD_CONV - 1) + k
        acc = acc + convw_ref[k:k + 1, :] * xbuf_ref[start:start + tm, :]
    xbuf_ref[0:halo, :] = xbc_pre[tm - halo:, :]
    xbc = acc * _sigmoid(acc)
    xs = xbc[:, :SSD_D_INNER]
    b_mat = xbc[:, SSD_D_INNER:SSD_D_INNER + BC_DIM]
    c_mat = xbc[:, SSD_D_INNER + BC_DIM:]

    dt_in = dt_raw + dtb_ref[...]
    dt = jnp.maximum(dt_in, 0.0) + jnp.log1p(jnp.exp(-jnp.abs(dt_in)))
    a_row = -jnp.exp(alog_ref[...])
    y = _ssd_chunk(xs, b_mat, c_mat, dt, a_row, s_ref)
    y = y + dskip_ref[...] * xs
    y = y * (z * _sigmoid(z))
    gsz = SSD_D_INNER // SSD_GROUPS
    parts = []
    for g in range(SSD_GROUPS):
        yg = y[:, g * gsz:(g + 1) * gsz]
        ms = jnp.mean(yg * yg, axis=-1, keepdims=True)
        parts.append(yg * lax.rsqrt(ms + RMS_EPS))
    y = jnp.concatenate(parts, axis=-1) * normw_ref[...]
    y_ref[...] = y.astype(_BF16)


def _mixer_front(h3d, sinks, w_front, conv_w, conv_b, dt_bias, a_log, d_skip_row, norm_w):
    bsz, seqlen, _ = h3d.shape
    tm = SEQ_TILE
    tok = lambda width: pl.BlockSpec((None, tm, width), lambda b, i: (b, i, 0))
    return pl.pallas_call(
        _mixer_front_kernel,
        grid=(bsz, seqlen // tm),
        in_specs=[pl.BlockSpec(memory_space=pltpu.SMEM),
                  tok(D_MODEL),
                  _resident((D_MODEL, FRONT_DIM)),
                  _resident((SSD_CONV, CONV_DIM)), _resident((1, CONV_DIM)),
                  _resident((1, LANES)), _resident((1, LANES)),
                  _resident((1, SSD_D_INNER)), _resident((1, SSD_D_INNER))],
        out_specs=[tok(SSD_D_INNER), tok(Q_DIM)],
        out_shape=[jax.ShapeDtypeStruct((bsz, seqlen, SSD_D_INNER), _BF16),
                   jax.ShapeDtypeStruct((bsz, seqlen, Q_DIM), _BF16)],
        scratch_shapes=[pltpu.VMEM((SSD_HEADS, SSD_STATE, SSD_HEAD_DIM), _F32),
                        pltpu.VMEM((ATT_BLOCK, 2 * KV_DIM), _BF16),
                        pltpu.VMEM((SUBLANES + tm, CONV_DIM), _F32)],
        compiler_params=pltpu.CompilerParams(dimension_semantics=("arbitrary", "arbitrary"),
                                             vmem_limit_bytes=VMEM_LIMIT),
        name="mixer_front",
    )(sinks, h3d, w_front, conv_w, conv_b, dt_bias, a_log, d_skip_row, norm_w)


def _mixer_back_kernel(h_ref, y_ref, att_ref, wg_ref, wa_ref, wb_ref, wm_ref, g_ref, b_ref, o_ref):
    h = h_ref[...]
    gates = _sigmoid(_dot(h.astype(_BF16), wg_ref[...]))
    y_a = _dot(y_ref[...], wa_ref[...])
    y_b = _dot(att_ref[...], wb_ref[...])
    merged = gates[:, :D_MODEL] * y_a + gates[:, D_MODEL:] * y_b
    mix = _dot(merged.astype(_BF16), wm_ref[...])
    o_ref[...] = _layer_norm(DEEPNORM_ALPHA * h + mix, g_ref[...], b_ref[...])


def _mixer_back(h2d, y2d, att2d, w_gates, w_ssd_out, w_att_out, w_mix_out, ln_g, ln_b):
    n = h2d.shape[0]
    rows = lambda width: pl.BlockSpec((ROW_TILE, width), lambda i: (i, 0))
    return pl.pallas_call(
        _mixer_back_kernel,
        grid=(n // ROW_TILE,),
        in_specs=[rows(D_MODEL), rows(SSD_D_INNER), rows(Q_DIM),
                  _resident((D_MODEL, 2 * D_MODEL)), _resident((SSD_D_INNER, D_MODEL)),
                  _resident((Q_DIM, D_MODEL)), _resident((D_MODEL, D_MODEL)),
                  _resident((1, D_MODEL)), _resident((1, D_MODEL))],
        out_specs=rows(D_MODEL),
        out_shape=jax.ShapeDtypeStruct((n, D_MODEL), _F32),
        compiler_params=pltpu.CompilerParams(dimension_semantics=("arbitrary",),
                                             vmem_limit_bytes=VMEM_LIMIT),
        name="mixer_back",
    )(h2d, y2d, att2d, w_gates, w_ssd_out, w_att_out, w_mix_out, ln_g, ln_b)


def _ffn_kernel(h_ref, wg_ref, wu_ref, wd_ref, g_ref, b_ref, o_ref):
    h = h_ref[...]
    hb = h.astype(_BF16)
    gate = _dot(hb, wg_ref[...])
    up = _dot(hb, wu_ref[...])
    act = (gate * _sigmoid(gate) * up).astype(_BF16)
    ffn = _dot(act, wd_ref[...])
    o_ref[...] = _layer_norm(DEEPNORM_ALPHA * h + ffn, g_ref[...], b_ref[...])


def _ffn(h2d, w_gate, w_up, w_down, ln_g, ln_b):
    n = h2d.shape[0]
    rows = pl.BlockSpec((ROW_TILE, D_MODEL), lambda i: (i, 0))
    return pl.pallas_call(
        _ffn_kernel,
        grid=(n // ROW_TILE,),
        in_specs=[rows, _resident((D_MODEL, FFN_HIDDEN)), _resident((D_MODEL, FFN_HIDDEN)),
                  _resident((FFN_HIDDEN, D_MODEL)), _resident((1, D_MODEL)), _resident((1, D_MODEL))],
        out_specs=rows,
        out_shape=jax.ShapeDtypeStruct((n, D_MODEL), _F32),
        compiler_params=pltpu.CompilerParams(dimension_semantics=("arbitrary",),
                                             vmem_limit_bytes=VMEM_LIMIT),
        name="ffn",
    )(h2d, w_gate, w_up, w_down, ln_g, ln_b)


def kernel(x, ln_in_g, ln_in_b, w_in, conv_w, conv_b, dt_bias, a_log, d_skip, ssd_norm_w, att_sinks,
           w_ssd_out, w_att_out, w_mix_out, ln_mix_g, ln_mix_b, w_ffn_gate, w_ffn_up, w_ffn_down,
           ln_ffn_g, ln_ffn_b):
    bsz, seqlen, _ = x.shape
    n = bsz * seqlen
    row = lambda v: v.reshape(1, -1)
    pad_heads = lambda v: jnp.pad(v, (0, LANES - SSD_HEADS)).reshape(1, LANES)

    h = _input_layer_norm(x.reshape(n, D_MODEL), row(ln_in_g), row(ln_in_b))
    for l in range(DEPTH):
        w_front = jnp.pad(w_in[l, :, :OFF_GATES], ((0, 0), (0, LANES - SSD_HEADS))).astype(_BF16)
        w_gates = w_in[l, :, OFF_GATES:].astype(_BF16)
        y, att = _mixer_front(
            h.reshape(bsz, seqlen, D_MODEL), att_sinks[l], w_front, conv_w[l], row(conv_b[l]),
            pad_heads(dt_bias[l]), pad_heads(a_log[l]),
            row(jnp.repeat(d_skip[l], SSD_HEAD_DIM)), row(ssd_norm_w[l]))
        h = _mixer_back(h, y.reshape(n, SSD_D_INNER), att.reshape(n, Q_DIM), w_gates,
                        w_ssd_out[l].astype(_BF16), w_att_out[l].astype(_BF16),
                        w_mix_out[l].astype(_BF16), row(ln_mix_g[l]), row(ln_mix_b[l]))
        h = _ffn(h, w_ffn_gate[l].astype(_BF16), w_ffn_up[l].astype(_BF16),
                 w_ffn_down[l].astype(_BF16), row(ln_ffn_g[l]), row(ln_ffn_b[l]))
    return h.reshape(bsz, seqlen, D_MODEL)
```

```python
import functools

import jax
import jax.numpy as jnp
from jax import lax
from jax.experimental import pallas as pl
from jax.experimental.pallas import tpu as pltpu

D_MODEL = 1024
ATT_HEADS = 16
ATT_KV_HEADS = 2
ATT_HEAD_DIM = 64
ATT_BLOCK = 128
SSD_D_INNER = 2048
SSD_HEAD_DIM = 64
SSD_HEADS = 32
SSD_GROUPS = 4
SSD_HEADS_PER_GROUP = SSD_HEADS // SSD_GROUPS
SSD_STATE = 128
SSD_CONV = 4
SSD_CHUNK = 128
FFN_HIDDEN = 2816
DEPTH = 2
LN_EPS = 1e-5
RMS_EPS = 1e-5
DEEPNORM_ALPHA = (2 * DEPTH) ** 0.25
LOG2E = 1.4426950408889634

Q_DIM = ATT_HEADS * ATT_HEAD_DIM
KV_DIM = ATT_KV_HEADS * ATT_HEAD_DIM
BC_DIM = SSD_GROUPS * SSD_STATE
CONV_DIM = SSD_D_INNER + 2 * BC_DIM

OFF_Q = 0
OFF_KV = OFF_Q + Q_DIM
OFF_Z = OFF_KV + 2 * KV_DIM
OFF_XBC = OFF_Z + SSD_D_INNER
OFF_DT = OFF_XBC + CONV_DIM
LANES = 128
SUBLANES = 8
FRONT_DIM = OFF_DT + LANES
OFF_GATES = OFF_DT + SSD_HEADS

SEQ_TILE = 128
TIME_GROUPS = SEQ_TILE // SUBLANES
ROW_TILE = 512
VMEM_LIMIT = 56 * 1024 * 1024

_F32 = jnp.float32
_BF16 = jnp.bfloat16
_NT = (((1,), (1,)), ((), ()))


def _dot(a, b):
    return jnp.dot(a, b, preferred_element_type=_F32)


def _sigmoid(x):
    return 1.0 / (1.0 + jnp.exp2(x * (-LOG2E)))


def _block_time(idx):
    return (idx & (SUBLANES - 1)) * TIME_GROUPS + idx // SUBLANES


def _strided_row_groups(ref, chunk_ref, groups):
    chunks = chunk_ref.shape[0]
    for c in range(chunks):
        chunk_ref[c] = ref[:, c * LANES:(c + 1) * LANES]
    pieces = []
    for start, stride in groups:
        pieces.append(jnp.concatenate(
            [chunk_ref[c, pl.ds(start, SUBLANES, stride=stride), :] for c in range(chunks)], axis=1))
    return jnp.concatenate(pieces, axis=0)


def _load_time_major_to_sublane_major(ref, chunk_ref):
    groups = [(blk * SEQ_TILE + v, TIME_GROUPS)
              for blk in range(ref.shape[0] // SEQ_TILE) for v in range(TIME_GROUPS)]
    return _strided_row_groups(ref, chunk_ref, groups)


def _load_sublane_major_to_time_major(ref, chunk_ref):
    groups = [(blk * SEQ_TILE + (a % 2) * (SEQ_TILE // 2) + a // 2, SUBLANES)
              for blk in range(ref.shape[0] // SEQ_TILE) for a in range(TIME_GROUPS)]
    return _strided_row_groups(ref, chunk_ref, groups)


def _layer_norm(x, g, b):
    mu = jnp.mean(x, axis=-1, keepdims=True)
    xc = x - mu
    var = jnp.mean(xc * xc, axis=-1, keepdims=True)
    return xc * lax.rsqrt(var + LN_EPS) * g + b


def _resident(shape):
    nd = len(shape)
    return pl.BlockSpec(shape, lambda *_: (0,) * nd, pipeline_mode=pl.Buffered(1))


def _ln_kernel(x_ref, g_ref, b_ref, o_ref, chunk_ref):
    x = _load_time_major_to_sublane_major(x_ref, chunk_ref)
    o_ref[...] = _layer_norm(x, g_ref[...], b_ref[...])


def _input_layer_norm(x2d, g, b):
    n = x2d.shape[0]
    return pl.pallas_call(
        _ln_kernel,
        grid=(n // ROW_TILE,),
        in_specs=[pl.BlockSpec((ROW_TILE, D_MODEL), lambda i: (i, 0)),
                  _resident((1, D_MODEL)), _resident((1, D_MODEL))],
        out_specs=pl.BlockSpec((ROW_TILE, D_MODEL), lambda i: (i, 0)),
        out_shape=jax.ShapeDtypeStruct((n, D_MODEL), _F32),
        scratch_shapes=[pltpu.VMEM((D_MODEL // LANES, ROW_TILE, LANES), _F32)],
        compiler_params=pltpu.CompilerParams(dimension_semantics=("arbitrary",)),
        name="input_ln",
    )(x2d, g, b)


def _attention_phases(q, kv_prev, kv_cur, sinks_ref, first_block, out_list):
    blk = ATT_BLOCK
    half = ATT_HEAD_DIM
    band = jnp.concatenate([kv_prev, kv_cur], axis=0)
    k_tile, v_tile = band[:, :KV_DIM], band[:, KV_DIM:]
    swap = lambda t: jnp.concatenate([t[:, half:], t[:, :half]], axis=1)
    k_swap, v_swap = swap(k_tile), swap(v_tile)
    lane_kv = lax.broadcasted_iota(jnp.int32, (2 * blk, KV_DIM), 1)
    low_kv = lane_kv < half
    zero = jnp.zeros((2 * blk, KV_DIM), _F32)
    k_low = (jnp.where(low_kv, k_tile, zero).astype(_BF16), jnp.where(low_kv, k_swap, zero).astype(_BF16))
    k_high = (jnp.where(low_kv, zero, k_swap).astype(_BF16), jnp.where(low_kv, zero, k_tile).astype(_BF16))
    ones = jnp.ones((2 * blk, KV_DIM), _BF16)
    v_ext = (jnp.concatenate([jnp.where(low_kv, v_tile, v_swap).astype(_BF16), ones], axis=1),
             jnp.concatenate([jnp.where(low_kv, v_swap, v_tile).astype(_BF16), ones], axis=1))

    row = lax.broadcasted_iota(jnp.int32, (blk, 2 * blk), 0)
    col = lax.broadcasted_iota(jnp.int32, (blk, 2 * blk), 1)
    key_time = _block_time(col & (blk - 1)) + (col & blk)
    rel = _block_time(row) + blk - key_time
    first_key = jnp.where(first_block, blk, 0)
    valid = (rel >= 0) & (rel < blk) & (col >= first_key)
    rel_f = rel.astype(_F32)
    low_out = lax.broadcasted_iota(jnp.int32, (blk, 2 * half), 1) < half
    group = ATT_HEADS // ATT_KV_HEADS
    scores = []
    for h in range(ATT_HEADS):
        q_pair = q[:, (h // 2) * 2 * half:(h // 2 + 1) * 2 * half]
        keys = (k_low if h % 2 == 0 else k_high)[h // group]
        scores.append(lax.dot_general(q_pair, keys, _NT, preferred_element_type=_F32))
    yield
    probs, sink_terms = [], []
    for h in range(ATT_HEADS):
        slope = LOG2E * 2.0 ** (-8.0 * (h + 1) / ATT_HEADS)
        s = jnp.where(valid, scores[h] - slope * rel_f, -jnp.inf)
        sink = sinks_ref[h] * LOG2E
        m = jnp.maximum(jnp.max(s, axis=-1, keepdims=True), sink)
        probs.append(jnp.exp2(s - m).astype(_BF16))
        sink_terms.append(jnp.exp2(sink - m))
    yield
    outs = []
    for j in range(ATT_HEADS // 2):
        halves = []
        for h in (2 * j, 2 * j + 1):
            res = _dot(probs[h], v_ext[h // group])
            halves.append(res[:, :2 * half] / (res[:, 2 * half:] + sink_terms[h]))
        outs.append(jnp.where(low_out, halves[0], halves[1]))
    out_list.append(jnp.concatenate(outs, axis=-1))


def _ssd_phases(xs, b_mat, c_mat, dt, a_row, s_ref, out_list):
    t = SSD_CHUNK
    hp = SSD_HEAD_DIM
    d_a = dt * a_row
    d_a_hi = d_a.astype(_BF16)
    d_a_lo = (d_a - d_a_hi.astype(_F32)).astype(_BF16)
    r = lax.broadcasted_iota(jnp.int32, (t, t), 0)
    c = lax.broadcasted_iota(jnp.int32, (t, t), 1)
    causal = _block_time(r) >= _block_time(c)
    tri = jnp.where(causal, 1.0, 0.0).astype(_BF16)
    cum2 = (_dot(tri, d_a_hi) + _dot(tri, d_a_lo)) * LOG2E
    c_gs = [c_mat[:, g * SSD_STATE:(g + 1) * SSD_STATE] for g in range(SSD_GROUPS)]
    b_gs = [b_mat[:, g * SSD_STATE:(g + 1) * SSD_STATE] for g in range(SSD_GROUPS)]
    cbs = [lax.dot_general(c_gs[g].astype(_BF16), b_gs[g].astype(_BF16), _NT, preferred_element_type=_F32)
           for g in range(SSD_GROUPS)]
    yield
    cum2_t = cum2.T
    src2_t = cum2_t - (jnp.log(dt) * LOG2E).T
    w_state_t = jnp.exp2(cum2_t[:, t - 1:t] - src2_t)
    chunk_decay = jnp.exp2(cum2[t - 1:t, :])
    lhs, bw = [], []
    for g in range(SSD_GROUPS):
        b_g_t = b_gs[g].T
        for e in range(SSD_HEADS_PER_GROUP):
            h = g * SSD_HEADS_PER_GROUP + e
            dst2 = jnp.broadcast_to(cum2[:, h:h + 1], (t, t))
            seg2 = jnp.where(causal, dst2 - src2_t[h:h + 1, :], -jnp.inf)
            m_intra = jnp.exp2(seg2) * cbs[g]
            c_scaled = c_gs[g] * jnp.exp2(dst2)
            lhs.append(jnp.concatenate([m_intra, c_scaled], axis=1).astype(_BF16))
            bw.append((b_g_t * w_state_t[h:h + 1, :]).astype(_BF16))
    yield
    xs_b = xs.astype(_BF16)
    lane = lax.broadcasted_iota(jnp.int32, (t, 2 * hp), 1)
    low = lane < hp
    low_b = jnp.where(low, 1.0, 0.0).astype(_BF16)
    high_b = jnp.where(low, 0.0, 1.0).astype(_BF16)
    low_row = low[0:1, :]
    ys = []
    for j in range(SSD_HEADS // 2):
        x_pair = xs_b[:, 2 * j * hp:(2 * j + 2) * hp]
        state = s_ref[j]
        rhs = jnp.concatenate([x_pair, state.astype(_BF16)], axis=0)
        ys.append(jnp.where(low, _dot(lhs[2 * j], rhs), _dot(lhs[2 * j + 1], rhs)))
        x_split = jnp.concatenate([x_pair * low_b, x_pair * high_b], axis=0)
        decay = jnp.where(low_row, chunk_decay[:, 2 * j:2 * j + 1], chunk_decay[:, 2 * j + 1:2 * j + 2])
        s_ref[j] = state * decay + _dot(jnp.concatenate([bw[2 * j], bw[2 * j + 1]], axis=1), x_split)
    out_list.append(jnp.concatenate(ys, axis=-1))


def _mixer_front_kernel(sinks_ref, h_ref, w_ref, convw_ref, convb_ref, dtb_ref, alog_ref, dskip_ref,
                        normw_ref, y_ref, att_ref, s_ref, kv_ref, tail_ref, proj_ref):
    i = pl.program_id(1)
    tm = SEQ_TILE
    tail_rows = (SSD_CONV - 1) * SUBLANES

    def reset_carried_state():
        s_ref[...] = jnp.zeros_like(s_ref)
        kv_ref[...] = jnp.zeros_like(kv_ref)
        tail_ref[...] = jnp.zeros_like(tail_ref)

    @pl.when(i == 0)
    def _():
        proj_ref[...] = jnp.zeros_like(proj_ref)
        reset_carried_state()

    hb = h_ref[...].astype(_BF16)

    def project(lo, hi):
        proj_ref[:, lo:hi] = _dot(hb, w_ref[:, lo:hi])

    q = (proj_ref[:, OFF_Q:OFF_KV] * (LOG2E * ATT_HEAD_DIM ** -0.5)).astype(_BF16)
    kv_cur = proj_ref[:, OFF_KV:OFF_Z]
    z = proj_ref[:, OFF_Z:OFF_XBC]
    xbc_pre = proj_ref[:, OFF_XBC:OFF_DT]
    dt_raw = proj_ref[:, OFF_DT:FRONT_DIM]

    att_out, ssd_out = [], []
    att = _attention_phases(q, kv_ref[...], kv_cur, sinks_ref, i == 1, att_out)
    next(att)
    kv_ref[...] = kv_cur
    z_gate = z * _sigmoid(z)
    project(OFF_Q, OFF_Z)
    project(OFF_Z, OFF_XBC)

    tail = xbc_pre[tm - tail_rows:, :]
    sublane = lax.broadcasted_iota(jnp.int32, (SUBLANES, CONV_DIM), 0)
    wrapped = []
    for j in range(SSD_CONV - 1):
        rows = slice(j * SUBLANES, (j + 1) * SUBLANES)
        mixed = jnp.where(sublane == SUBLANES - 1, tail_ref[rows, :], tail[rows, :])
        wrapped.append(pltpu.roll(mixed, 1, axis=0))
    ext = jnp.concatenate(wrapped + [xbc_pre], axis=0)
    tail_ref[...] = tail
    acc = convb_ref[...]
    for k in range(SSD_CONV):
        acc = acc + convw_ref[k:k + 1, :] * ext[k * SUBLANES:k * SUBLANES + tm, :]
    xbc = acc * _sigmoid(acc)
    xs = xbc[:, :SSD_D_INNER]
    b_mat = xbc[:, SSD_D_INNER:SSD_D_INNER + BC_DIM]
    c_mat = xbc[:, SSD_D_INNER + BC_DIM:]
    dt_in = dt_raw + dtb_ref[...]
    dt = jnp.maximum(dt_in, 0.0) + jnp.log1p(jnp.exp(-jnp.abs(dt_in)))
    a_row = -jnp.exp(alog_ref[...])
    ssd = _ssd_phases(xs, b_mat, c_mat, dt, a_row, s_ref, ssd_out)
    next(ssd)
    project(OFF_XBC, FRONT_DIM)

    next(att)
    next(ssd)
    for phases in (att, ssd):
        for _ in phases:
            pass
    att_ref[...] = att_out[0].astype(_BF16)
    y = ssd_out[0] + dskip_ref[...] * xs
    y = y * z_gate
    gsz = SSD_D_INNER // SSD_GROUPS
    parts = []
    for g in range(SSD_GROUPS):
        yg = y[:, g * gsz:(g + 1) * gsz]
        ms = jnp.mean(yg * yg, axis=-1, keepdims=True)
        parts.append(yg * lax.rsqrt(ms + RMS_EPS))
    y = jnp.concatenate(parts, axis=-1) * normw_ref[...]
    y_ref[...] = y.astype(_BF16)

    @pl.when(i == 0)
    def _():
        reset_carried_state()


def _mixer_front(h3d, sinks, w_front, conv_w, conv_b, dt_bias, a_log, d_skip_row, norm_w):
    bsz, seqlen, _ = h3d.shape
    tm = SEQ_TILE
    nblk = seqlen // tm
    tok = lambda width: pl.BlockSpec((None, tm, width), lambda b, i: (b, jnp.maximum(i - 1, 0), 0))
    return pl.pallas_call(
        _mixer_front_kernel,
        grid=(bsz, nblk + 1),
        in_specs=[pl.BlockSpec(memory_space=pltpu.SMEM),
                  pl.BlockSpec((None, tm, D_MODEL), lambda b, i: (b, jnp.minimum(i, nblk - 1), 0)),
                  _resident((D_MODEL, FRONT_DIM)),
                  _resident((SSD_CONV, CONV_DIM)), _resident((1, CONV_DIM)),
                  _resident((1, LANES)), _resident((1, LANES)),
                  _resident((1, SSD_D_INNER)), _resident((1, SSD_D_INNER))],
        out_specs=[tok(SSD_D_INNER), tok(Q_DIM)],
        out_shape=[jax.ShapeDtypeStruct((bsz, seqlen, SSD_D_INNER), _BF16),
                   jax.ShapeDtypeStruct((bsz, seqlen, Q_DIM), _BF16)],
        scratch_shapes=[pltpu.VMEM((SSD_HEADS // 2, SSD_STATE, 2 * SSD_HEAD_DIM), _F32),
                        pltpu.VMEM((ATT_BLOCK, 2 * KV_DIM), _F32),
                        pltpu.VMEM(((SSD_CONV - 1) * SUBLANES, CONV_DIM), _F32),
                        pltpu.VMEM((tm, FRONT_DIM), _F32)],
        compiler_params=pltpu.CompilerParams(dimension_semantics=("arbitrary", "arbitrary"),
                                             vmem_limit_bytes=VMEM_LIMIT),
        name="mixer_front",
    )(sinks, h3d, w_front, conv_w, conv_b, dt_bias, a_log, d_skip_row, norm_w)


def _mixer_back_kernel(h_ref, y_ref, att_ref, wg_ref, wa_ref, wb_ref, wm_ref, g_ref, b_ref, o_ref):
    h = h_ref[...]
    gates = _sigmoid(_dot(h.astype(_BF16), wg_ref[...]))
    y_a = _dot(y_ref[...], wa_ref[...])
    y_b = _dot(att_ref[...], wb_ref[...])
    merged = gates[:, :D_MODEL] * y_a + gates[:, D_MODEL:] * y_b
    mix = _dot(merged.astype(_BF16), wm_ref[...])
    o_ref[...] = _layer_norm(DEEPNORM_ALPHA * h + mix, g_ref[...], b_ref[...])


def _mixer_back(h2d, y2d, att2d, w_gates, w_ssd_out, w_att_out, w_mix_out, ln_g, ln_b):
    n = h2d.shape[0]
    rows = lambda width: pl.BlockSpec((ROW_TILE, width), lambda i: (i, 0))
    return pl.pallas_call(
        _mixer_back_kernel,
        grid=(n // ROW_TILE,),
        in_specs=[rows(D_MODEL), rows(SSD_D_INNER), rows(Q_DIM),
                  _resident((D_MODEL, 2 * D_MODEL)), _resident((SSD_D_INNER, D_MODEL)),
                  _resident((Q_DIM, D_MODEL)), _resident((D_MODEL, D_MODEL)),
                  _resident((1, D_MODEL)), _resident((1, D_MODEL))],
        out_specs=rows(D_MODEL),
        out_shape=jax.ShapeDtypeStruct((n, D_MODEL), _F32),
        compiler_params=pltpu.CompilerParams(dimension_semantics=("arbitrary",),
                                             vmem_limit_bytes=VMEM_LIMIT),
        name="mixer_back",
    )(h2d, y2d, att2d, w_gates, w_ssd_out, w_att_out, w_mix_out, ln_g, ln_b)


def _ffn_kernel(h_ref, wg_ref, wu_ref, wd_ref, g_ref, b_ref, o_ref, *scratch, restore_time_order):
    h = _load_sublane_major_to_time_major(h_ref, scratch[0]) if restore_time_order else h_ref[...]
    hb = h.astype(_BF16)
    gate = _dot(hb, wg_ref[...])
    up = _dot(hb, wu_ref[...])
    act = (gate * _sigmoid(gate) * up).astype(_BF16)
    ffn = _dot(act, wd_ref[...])
    o_ref[...] = _layer_norm(DEEPNORM_ALPHA * h + ffn, g_ref[...], b_ref[...])


def _ffn(h2d, w_gate, w_up, w_down, ln_g, ln_b, restore_time_order):
    n = h2d.shape[0]
    rows = pl.BlockSpec((ROW_TILE, D_MODEL), lambda i: (i, 0))
    return pl.pallas_call(
        functools.partial(_ffn_kernel, restore_time_order=restore_time_order),
        grid=(n // ROW_TILE,),
        in_specs=[rows, _resident((D_MODEL, FFN_HIDDEN)), _resident((D_MODEL, FFN_HIDDEN)),
                  _resident((FFN_HIDDEN, D_MODEL)), _resident((1, D_MODEL)), _resident((1, D_MODEL))],
        out_specs=rows,
        out_shape=jax.ShapeDtypeStruct((n, D_MODEL), _F32),
        scratch_shapes=[pltpu.VMEM((D_MODEL // LANES, ROW_TILE, LANES), _F32)] if restore_time_order else [],
        compiler_params=pltpu.CompilerParams(dimension_semantics=("arbitrary",),
                                             vmem_limit_bytes=VMEM_LIMIT),
        name="ffn",
    )(h2d, w_gate, w_up, w_down, ln_g, ln_b)


def kernel(x, ln_in_g, ln_in_b, w_in, conv_w, conv_b, dt_bias, a_log, d_skip, ssd_norm_w, att_sinks,
           w_ssd_out, w_att_out, w_mix_out, ln_mix_g, ln_mix_b, w_ffn_gate, w_ffn_up, w_ffn_down,
           ln_ffn_g, ln_ffn_b):
    bsz, seqlen, _ = x.shape
    n = bsz * seqlen
    row = lambda v: v.reshape(1, -1)
    pad_heads = lambda v: jnp.pad(v, (0, LANES - SSD_HEADS)).reshape(1, LANES)

    h = _input_layer_norm(x.reshape(n, D_MODEL), row(ln_in_g), row(ln_in_b))
    for l in range(DEPTH):
        w_front = jnp.pad(w_in[l, :, :OFF_GATES], ((0, 0), (0, LANES - SSD_HEADS))).astype(_BF16)
        w_gates = w_in[l, :, OFF_GATES:].astype(_BF16)
        y, att = _mixer_front(
            h.reshape(bsz, seqlen, D_MODEL), att_sinks[l], w_front, conv_w[l], row(conv_b[l]),
            pad_heads(dt_bias[l]), pad_heads(a_log[l]),
            row(jnp.repeat(d_skip[l], SSD_HEAD_DIM)), row(ssd_norm_w[l]))
        h = _mixer_back(h, y.reshape(n, SSD_D_INNER), att.reshape(n, Q_DIM), w_gates,
                        w_ssd_out[l].astype(_BF16), w_att_out[l].astype(_BF16),
                        w_mix_out[l].astype(_BF16), row(ln_mix_g[l]), row(ln_mix_b[l]))
        h = _ffn(h, w_ffn_gate[l].astype(_BF16), w_ffn_up[l].astype(_BF16),
                 w_ffn_down[l].astype(_BF16), row(ln_ffn_g[l]), row(ln_ffn_b[l]),
                 restore_time_order=(l == DEPTH - 1))
    return h.reshape(bsz, seqlen, D_MODEL)
```

```python
import functools

import numpy as np
import jax
import jax.numpy as jnp
from jax import lax
from jax.experimental import pallas as pl
from jax.experimental.pallas import tpu as pltpu

D_MODEL = 1024
ATT_HEADS = 16
ATT_KV_HEADS = 2
ATT_HEAD_DIM = 64
ATT_BLOCK = 128
SSD_D_INNER = 2048
SSD_HEAD_DIM = 64
SSD_HEADS = 32
SSD_GROUPS = 4
SSD_HEADS_PER_GROUP = SSD_HEADS // SSD_GROUPS
SSD_STATE = 128
SSD_CONV = 4
SSD_CHUNK = 128
FFN_HIDDEN = 2816
DEPTH = 2
LN_EPS = 1e-5
RMS_EPS = 1e-5
DEEPNORM_ALPHA = (2 * DEPTH) ** 0.25
LOG2E = 1.4426950408889634

Q_DIM = ATT_HEADS * ATT_HEAD_DIM
KV_DIM = ATT_KV_HEADS * ATT_HEAD_DIM
BC_DIM = SSD_GROUPS * SSD_STATE
CONV_DIM = SSD_D_INNER + 2 * BC_DIM

OFF_Q = 0
OFF_KV = OFF_Q + Q_DIM
OFF_Z = OFF_KV + 2 * KV_DIM
OFF_XBC = OFF_Z + SSD_D_INNER
OFF_DT = OFF_XBC + CONV_DIM
LANES = 128
SUBLANES = 8
FRONT_DIM = OFF_DT + LANES
OFF_GATES = OFF_DT + SSD_HEADS

ALIBI_LANES = 6
SEQ_TILE = 128
TIME_GROUPS = SEQ_TILE // SUBLANES
ROW_TILE = 512
VMEM_LIMIT = 56 * 1024 * 1024

_F32 = jnp.float32
_BF16 = jnp.bfloat16
_NT = (((1,), (1,)), ((), ()))


def _dot(a, b):
    return jnp.dot(a, b, preferred_element_type=_F32)


def _sigmoid(x):
    return 1.0 / (1.0 + jnp.exp2(x * (-LOG2E)))


def _block_time(idx):
    return (idx & (SUBLANES - 1)) * TIME_GROUPS + idx // SUBLANES


def _strided_row_groups(ref, chunk_ref, groups):
    chunks = chunk_ref.shape[0]
    for c in range(chunks):
        chunk_ref[c] = ref[:, c * LANES:(c + 1) * LANES]
    pieces = []
    for start, stride in groups:
        pieces.append(jnp.concatenate(
            [chunk_ref[c, pl.ds(start, SUBLANES, stride=stride), :] for c in range(chunks)], axis=1))
    return jnp.concatenate(pieces, axis=0)


def _load_time_major_to_sublane_major(ref, chunk_ref):
    groups = [(blk * SEQ_TILE + v, TIME_GROUPS)
              for blk in range(ref.shape[0] // SEQ_TILE) for v in range(TIME_GROUPS)]
    return _strided_row_groups(ref, chunk_ref, groups)


def _load_sublane_major_to_time_major(ref, chunk_ref):
    groups = [(blk * SEQ_TILE + (a % 2) * (SEQ_TILE // 2) + a // 2, SUBLANES)
              for blk in range(ref.shape[0] // SEQ_TILE) for a in range(TIME_GROUPS)]
    return _strided_row_groups(ref, chunk_ref, groups)


def _layer_norm(x, g, b):
    mu = jnp.mean(x, axis=-1, keepdims=True)
    xc = x - mu
    var = jnp.mean(xc * xc, axis=-1, keepdims=True)
    return xc * lax.rsqrt(var + LN_EPS) * g + b


def _resident(shape):
    nd = len(shape)
    return pl.BlockSpec(shape, lambda *_: (0,) * nd, pipeline_mode=pl.Buffered(1))


def _ln_kernel(x_ref, g_ref, b_ref, o_ref, chunk_ref):
    x = _load_time_major_to_sublane_major(x_ref, chunk_ref)
    o_ref[...] = _layer_norm(x, g_ref[...], b_ref[...])


def _input_layer_norm(x2d, g, b):
    n = x2d.shape[0]
    return pl.pallas_call(
        _ln_kernel,
        grid=(n // ROW_TILE,),
        in_specs=[pl.BlockSpec((ROW_TILE, D_MODEL), lambda i: (i, 0)),
                  _resident((1, D_MODEL)), _resident((1, D_MODEL))],
        out_specs=pl.BlockSpec((ROW_TILE, D_MODEL), lambda i: (i, 0)),
        out_shape=jax.ShapeDtypeStruct((n, D_MODEL), _F32),
        scratch_shapes=[pltpu.VMEM((D_MODEL // LANES, ROW_TILE, LANES), _F32)],
        compiler_params=pltpu.CompilerParams(dimension_semantics=("arbitrary",)),
        name="input_ln",
    )(x2d, g, b)


def _alibi_operands():
    def split3(v):
        parts, rest = [], np.asarray(v, np.float64)
        for _ in range(3):
            p = rest.astype(jnp.bfloat16)
            parts.append(p)
            rest = rest - p.astype(np.float64)
        return parts

    idx = np.arange(ATT_BLOCK)
    q_time = (idx % SUBLANES) * TIME_GROUPS + idx // SUBLANES
    key_idx = np.arange(2 * ATT_BLOCK)
    k_idx = key_idx % ATT_BLOCK
    key_time = (k_idx % SUBLANES) * TIME_GROUPS + k_idx // SUBLANES + (key_idx // ATT_BLOCK) * ATT_BLOCK
    q_side = np.zeros((ATT_HEADS // 2, ATT_BLOCK, LANES), jnp.bfloat16)
    k_side = np.zeros((2, 2 * ATT_BLOCK, LANES), jnp.bfloat16)
    for h in range(ATT_HEADS):
        slope = LOG2E * 2.0 ** (-8.0 * (h + 1) / ATT_HEADS)
        lane0 = ALIBI_LANES * (h % 2)
        for t, part in enumerate(split3(np.full(ATT_BLOCK, slope))):
            q_side[h // 2, :, lane0 + t] = part
        for t, part in enumerate(split3(-slope * (q_time + ATT_BLOCK))):
            q_side[h // 2, :, lane0 + 3 + t] = part
    for parity in range(2):
        lane0 = ALIBI_LANES * parity
        k_side[parity, :, lane0:lane0 + 3] = key_time[:, None].astype(jnp.bfloat16)
        k_side[parity, :, lane0 + 3:lane0 + 6] = 1.0
    return jnp.asarray(q_side), jnp.asarray(k_side)


def _attention_phases(q, kv_prev, kv_cur, sinks_ref, alibi_q_ref, alibi_k_ref, first_block, out_list):
    blk = ATT_BLOCK
    half = ATT_HEAD_DIM
    band = jnp.concatenate([kv_prev, kv_cur], axis=0)
    k_tile, v_tile = band[:, :KV_DIM], band[:, KV_DIM:]
    swap = lambda t: jnp.concatenate([t[:, half:], t[:, :half]], axis=1)
    k_swap, v_swap = swap(k_tile), swap(v_tile)
    lane_kv = lax.broadcasted_iota(jnp.int32, (2 * blk, KV_DIM), 1)
    low_kv = lane_kv < half
    zero = jnp.zeros((2 * blk, KV_DIM), _F32)
    k_low = (jnp.where(low_kv, k_tile, zero).astype(_BF16), jnp.where(low_kv, k_swap, zero).astype(_BF16))
    k_high = (jnp.where(low_kv, zero, k_swap).astype(_BF16), jnp.where(low_kv, zero, k_tile).astype(_BF16))
    ones = jnp.ones((2 * blk, KV_DIM), _BF16)
    v_ext = (jnp.concatenate([jnp.where(low_kv, v_tile, v_swap).astype(_BF16), ones], axis=1),
             jnp.concatenate([jnp.where(low_kv, v_swap, v_tile).astype(_BF16), ones], axis=1))

    row = lax.broadcasted_iota(jnp.int32, (blk, 2 * blk), 0)
    col = lax.broadcasted_iota(jnp.int32, (blk, 2 * blk), 1)
    key_time = _block_time(col & (blk - 1)) + (col & blk)
    rel = _block_time(row) + blk - key_time
    first_key = jnp.where(first_block, blk, 0)
    valid = (rel >= 0) & (rel < blk) & (col >= first_key)
    low_out = lax.broadcasted_iota(jnp.int32, (blk, 2 * half), 1) < half
    group = ATT_HEADS // ATT_KV_HEADS
    scores = []
    for h in range(ATT_HEADS):
        j = h // 2
        q_pair = jnp.concatenate([q[:, j * 2 * half:(j + 1) * 2 * half], alibi_q_ref[j]], axis=1)
        keys = jnp.concatenate([(k_low if h % 2 == 0 else k_high)[h // group], alibi_k_ref[h % 2]], axis=1)
        scores.append(lax.dot_general(q_pair, keys, _NT, preferred_element_type=_F32))
    yield
    probs, sink_terms = [], []
    for h in range(ATT_HEADS):
        s = jnp.where(valid, scores[h], -jnp.inf)
        sink = sinks_ref[h] * LOG2E
        m = jnp.maximum(jnp.max(s, axis=-1, keepdims=True), sink)
        probs.append(jnp.exp2(s - m).astype(_BF16))
        sink_terms.append(jnp.exp2(sink - m))
    yield
    outs = []
    for j in range(ATT_HEADS // 2):
        halves = []
        for h in (2 * j, 2 * j + 1):
            res = _dot(probs[h], v_ext[h // group])
            halves.append(res[:, :2 * half] / (res[:, 2 * half:] + sink_terms[h]))
        outs.append(jnp.where(low_out, halves[0], halves[1]))
    out_list.append(jnp.concatenate(outs, axis=-1))


def _ssd_phases(xs, b_mat, c_mat, dt, a_row, s_ref, out_list):
    t = SSD_CHUNK
    hp = SSD_HEAD_DIM
    d_a = dt * a_row
    d_a_hi = d_a.astype(_BF16)
    d_a_lo = (d_a - d_a_hi.astype(_F32)).astype(_BF16)
    r = lax.broadcasted_iota(jnp.int32, (t, t), 0)
    c = lax.broadcasted_iota(jnp.int32, (t, t), 1)
    causal = _block_time(r) >= _block_time(c)
    tri = jnp.where(causal, 1.0, 0.0).astype(_BF16)
    cum2 = (_dot(tri, d_a_hi) + _dot(tri, d_a_lo)) * LOG2E
    c_gs = [c_mat[:, g * SSD_STATE:(g + 1) * SSD_STATE] for g in range(SSD_GROUPS)]
    b_gs = [b_mat[:, g * SSD_STATE:(g + 1) * SSD_STATE] for g in range(SSD_GROUPS)]
    cbs = [lax.dot_general(c_gs[g].astype(_BF16), b_gs[g].astype(_BF16), _NT, preferred_element_type=_F32)
           for g in range(SSD_GROUPS)]
    yield
    cum2_t = cum2.T
    src2_t = cum2_t - (jnp.log(dt) * LOG2E).T
    w_state_t = jnp.exp2(cum2_t[:, t - 1:t] - src2_t)
    chunk_decay = jnp.exp2(cum2[t - 1:t, :])
    lhs, bw = [], []
    for g in range(SSD_GROUPS):
        cb_b = cbs[g].astype(_BF16)
        c_g_b = c_gs[g].astype(_BF16)
        b_g_t_b = b_gs[g].T.astype(_BF16)
        for e in range(SSD_HEADS_PER_GROUP):
            h = g * SSD_HEADS_PER_GROUP + e
            dst2 = jnp.broadcast_to(cum2[:, h:h + 1], (t, t))
            seg2 = jnp.where(causal, dst2 - src2_t[h:h + 1, :], -jnp.inf)
            m_intra = jnp.exp2(seg2).astype(_BF16) * cb_b
            c_scaled = jnp.exp2(dst2).astype(_BF16) * c_g_b
            lhs.append(jnp.concatenate([m_intra, c_scaled], axis=1))
            w_row = jnp.broadcast_to(w_state_t[h:h + 1, :], (SSD_STATE, t)).astype(_BF16)
            bw.append(b_g_t_b * w_row)
    yield
    xs_b = xs.astype(_BF16)
    lane = lax.broadcasted_iota(jnp.int32, (t, 2 * hp), 1)
    low = lane < hp
    low_b = jnp.where(low, 1.0, 0.0).astype(_BF16)
    high_b = jnp.where(low, 0.0, 1.0).astype(_BF16)
    low_row = low[0:1, :]
    ys = []
    for j in range(SSD_HEADS // 2):
        x_pair = xs_b[:, 2 * j * hp:(2 * j + 2) * hp]
        state = s_ref[j]
        rhs = jnp.concatenate([x_pair, state.astype(_BF16)], axis=0)
        ys.append(jnp.where(low, _dot(lhs[2 * j], rhs), _dot(lhs[2 * j + 1], rhs)))
        x_split = jnp.concatenate([x_pair * low_b, x_pair * high_b], axis=0)
        decay = jnp.where(low_row, chunk_decay[:, 2 * j:2 * j + 1], chunk_decay[:, 2 * j + 1:2 * j + 2])
        s_ref[j] = state * decay + _dot(jnp.concatenate([bw[2 * j], bw[2 * j + 1]], axis=1), x_split)
    out_list.append(jnp.concatenate(ys, axis=-1))


def _mixer_front_kernel(sinks_ref, h_ref, w_ref, convw_ref, convb_ref, dtb_ref, alog_ref, dskip_ref,
                        normw_ref, alibi_q_ref, alibi_k_ref, y_ref, att_ref, s_ref, kv_ref, tail_ref, proj_ref,
                        *, blocks_per_seq):
    i = pl.program_id(0)
    tm = SEQ_TILE
    tail_rows = (SSD_CONV - 1) * SUBLANES
    starts_sequence = lax.rem(i - 1, blocks_per_seq) == 0

    @pl.when(i == 0)
    def _():
        proj_ref[...] = jnp.zeros_like(proj_ref)

    @pl.when(jnp.logical_or(i == 0, starts_sequence))
    def _():
        s_ref[...] = jnp.zeros_like(s_ref)
        kv_ref[...] = jnp.zeros_like(kv_ref)
        tail_ref[...] = jnp.zeros_like(tail_ref)

    hb = h_ref[...].astype(_BF16)

    def project(lo, hi):
        proj_ref[:, lo:hi] = _dot(hb, w_ref[:, lo:hi])

    q = proj_ref[:, OFF_Q:OFF_KV].astype(_BF16)
    kv_lane = lax.broadcasted_iota(jnp.int32, (1, 2 * KV_DIM), 1)
    kv_cur = proj_ref[:, OFF_KV:OFF_Z] * jnp.where(kv_lane < KV_DIM, LOG2E * ATT_HEAD_DIM ** -0.5, 1.0)
    z = proj_ref[:, OFF_Z:OFF_XBC]
    xbc_pre = proj_ref[:, OFF_XBC:OFF_DT]
    dt_raw = proj_ref[:, OFF_DT:FRONT_DIM]

    att_out, ssd_out = [], []
    att = _attention_phases(q, kv_ref[...], kv_cur, sinks_ref, alibi_q_ref, alibi_k_ref, starts_sequence, att_out)
    next(att)
    kv_ref[...] = kv_cur
    z_gate = z * _sigmoid(z)
    project(OFF_Q, OFF_Z)
    project(OFF_Z, OFF_XBC)

    tail = xbc_pre[tm - tail_rows:, :]
    sublane = lax.broadcasted_iota(jnp.int32, (SUBLANES, CONV_DIM), 0)
    wrapped = []
    for j in range(SSD_CONV - 1):
        rows = slice(j * SUBLANES, (j + 1) * SUBLANES)
        mixed = jnp.where(sublane == SUBLANES - 1, tail_ref[rows, :], tail[rows, :])
        wrapped.append(pltpu.roll(mixed, 1, axis=0))
    ext = jnp.concatenate(wrapped + [xbc_pre], axis=0)
    tail_ref[...] = tail
    acc = convb_ref[...]
    for k in range(SSD_CONV):
        acc = acc + convw_ref[k:k + 1, :] * ext[k * SUBLANES:k * SUBLANES + tm, :]
    xbc = acc * _sigmoid(acc)
    xs = xbc[:, :SSD_D_INNER]
    b_mat = xbc[:, SSD_D_INNER:SSD_D_INNER + BC_DIM]
    c_mat = xbc[:, SSD_D_INNER + BC_DIM:]
    dt_in = dt_raw + dtb_ref[...]
    dt = jnp.maximum(dt_in, 0.0) + jnp.log1p(jnp.exp(-jnp.abs(dt_in)))
    a_row = -jnp.exp(alog_ref[...])
    ssd = _ssd_phases(xs, b_mat, c_mat, dt, a_row, s_ref, ssd_out)
    next(ssd)
    project(OFF_XBC, FRONT_DIM)

    next(att)
    next(ssd)
    for phases in (att, ssd):
        for _ in phases:
            pass
    att_ref[...] = att_out[0].astype(_BF16)
    y = ssd_out[0] + dskip_ref[...] * xs
    y = y * z_gate
    gsz = SSD_D_INNER // SSD_GROUPS
    parts = []
    for g in range(SSD_GROUPS):
        yg = y[:, g * gsz:(g + 1) * gsz]
        ms = jnp.mean(yg * yg, axis=-1, keepdims=True)
        parts.append(yg * lax.rsqrt(ms + RMS_EPS))
    y = jnp.concatenate(parts, axis=-1) * normw_ref[...]
    y_ref[...] = y.astype(_BF16)


def _mixer_front(h2d, seqlen, sinks, w_all, layer, conv_w, conv_b, dt_bias, a_log, d_skip_row, norm_w):
    n = h2d.shape[0]
    tm = SEQ_TILE
    nblk = n // tm
    alibi_q, alibi_k = _alibi_operands()
    tok = lambda width: pl.BlockSpec((tm, width), lambda i: (jnp.maximum(i - 1, 0), 0))
    return pl.pallas_call(
        functools.partial(_mixer_front_kernel, blocks_per_seq=seqlen // tm),
        grid=(nblk + 1,),
        in_specs=[pl.BlockSpec(memory_space=pltpu.SMEM),
                  pl.BlockSpec((tm, D_MODEL), lambda i: (jnp.minimum(i, nblk - 1), 0)),
                  pl.BlockSpec((None, D_MODEL, FRONT_DIM), lambda i: (layer, 0, 0), pipeline_mode=pl.Buffered(1)),
                  _resident((SSD_CONV, CONV_DIM)), _resident((1, CONV_DIM)),
                  _resident((1, LANES)), _resident((1, LANES)),
                  _resident((1, SSD_D_INNER)), _resident((1, SSD_D_INNER)),
                  _resident(alibi_q.shape), _resident(alibi_k.shape)],
        out_specs=[tok(SSD_D_INNER), tok(Q_DIM)],
        out_shape=[jax.ShapeDtypeStruct((n, SSD_D_INNER), _BF16),
                   jax.ShapeDtypeStruct((n, Q_DIM), _BF16)],
        scratch_shapes=[pltpu.VMEM((SSD_HEADS // 2, SSD_STATE, 2 * SSD_HEAD_DIM), _F32),
                        pltpu.VMEM((ATT_BLOCK, 2 * KV_DIM), _F32),
                        pltpu.VMEM(((SSD_CONV - 1) * SUBLANES, CONV_DIM), _F32),
                        pltpu.VMEM((tm, FRONT_DIM), _F32)],
        compiler_params=pltpu.CompilerParams(dimension_semantics=("arbitrary",),
                                             vmem_limit_bytes=VMEM_LIMIT),
        name="mixer_front",
    )(sinks, h2d, w_all, conv_w, conv_b, dt_bias, a_log, d_skip_row, norm_w, alibi_q, alibi_k)


def _mixer_back_kernel(h_ref, y_ref, att_ref, wg_ref, wa_ref, wb_ref, wm_ref, g_ref, b_ref, o_ref):
    h = h_ref[...]
    gates = _sigmoid(_dot(h.astype(_BF16), wg_ref[...]))
    y_a = _dot(y_ref[...], wa_ref[...])
    y_b = _dot(att_ref[...], wb_ref[...])
    merged = gates[:, :D_MODEL] * y_a + gates[:, D_MODEL:] * y_b
    mix = _dot(merged.astype(_BF16), wm_ref[...])
    o_ref[...] = _layer_norm(DEEPNORM_ALPHA * h + mix, g_ref[...], b_ref[...])


def _mixer_back(h2d, y2d, att2d, w_gates, w_ssd_out, w_att_out, w_mix_out, ln_g, ln_b):
    n = h2d.shape[0]
    rows = lambda width: pl.BlockSpec((ROW_TILE, width), lambda i: (i, 0))
    return pl.pallas_call(
        _mixer_back_kernel,
        grid=(n // ROW_TILE,),
        in_specs=[rows(D_MODEL), rows(SSD_D_INNER), rows(Q_DIM),
                  _resident((D_MODEL, 2 * D_MODEL)), _resident((SSD_D_INNER, D_MODEL)),
                  _resident((Q_DIM, D_MODEL)), _resident((D_MODEL, D_MODEL)),
                  _resident((1, D_MODEL)), _resident((1, D_MODEL))],
        out_specs=rows(D_MODEL),
        out_shape=jax.ShapeDtypeStruct((n, D_MODEL), _F32),
        compiler_params=pltpu.CompilerParams(dimension_semantics=("arbitrary",),
                                             vmem_limit_bytes=VMEM_LIMIT),
        name="mixer_back",
    )(h2d, y2d, att2d, w_gates, w_ssd_out, w_att_out, w_mix_out, ln_g, ln_b)


def _ffn_kernel(h_ref, wg_ref, wu_ref, wd_ref, g_ref, b_ref, o_ref, *scratch, restore_time_order):
    h = _load_sublane_major_to_time_major(h_ref, scratch[0]) if restore_time_order else h_ref[...]
    hb = h.astype(_BF16)
    gate = _dot(hb, wg_ref[...])
    up = _dot(hb, wu_ref[...])
    act = (gate * _sigmoid(gate) * up).astype(_BF16)
    ffn = _dot(act, wd_ref[...])
    o_ref[...] = _layer_norm(DEEPNORM_ALPHA * h + ffn, g_ref[...], b_ref[...])


def _ffn(h2d, w_gate, w_up, w_down, ln_g, ln_b, restore_time_order):
    n = h2d.shape[0]
    rows = pl.BlockSpec((ROW_TILE, D_MODEL), lambda i: (i, 0))
    return pl.pallas_call(
        functools.partial(_ffn_kernel, restore_time_order=restore_time_order),
        grid=(n // ROW_TILE,),
        in_specs=[rows, _resident((D_MODEL, FFN_HIDDEN)), _resident((D_MODEL, FFN_HIDDEN)),
                  _resident((FFN_HIDDEN, D_MODEL)), _resident((1, D_MODEL)), _resident((1, D_MODEL))],
        out_specs=rows,
        out_shape=jax.ShapeDtypeStruct((n, D_MODEL), _F32),
        scratch_shapes=[pltpu.VMEM((D_MODEL // LANES, ROW_TILE, LANES), _F32)] if restore_time_order else [],
        compiler_params=pltpu.CompilerParams(dimension_semantics=("arbitrary",),
                                             vmem_limit_bytes=VMEM_LIMIT),
        name="ffn",
    )(h2d, w_gate, w_up, w_down, ln_g, ln_b)


def kernel(x, ln_in_g, ln_in_b, w_in, conv_w, conv_b, dt_bias, a_log, d_skip, ssd_norm_w, att_sinks,
           w_ssd_out, w_att_out, w_mix_out, ln_mix_g, ln_mix_b, w_ffn_gate, w_ffn_up, w_ffn_down,
           ln_ffn_g, ln_ffn_b):
    bsz, seqlen, _ = x.shape
    n = bsz * seqlen
    row = lambda v: v.reshape(1, -1)
    pad_heads = lambda v: jnp.pad(v, (0, LANES - SSD_HEADS)).reshape(1, LANES)

    h = _input_layer_norm(x.reshape(n, D_MODEL), row(ln_in_g), row(ln_in_b))
    w_in_b = w_in.astype(_BF16)
    for l in range(DEPTH):
        y, att = _mixer_front(
            h, seqlen, att_sinks[l], w_in_b, l, conv_w[l], row(conv_b[l]),
            pad_heads(dt_bias[l]), pad_heads(a_log[l]),
            row(jnp.repeat(d_skip[l], SSD_HEAD_DIM)), row(ssd_norm_w[l]))
        h = _mixer_back(h, y, att, w_in_b[l, :, OFF_GATES:],
                        w_ssd_out[l].astype(_BF16), w_att_out[l].astype(_BF16),
                        w_mix_out[l].astype(_BF16), row(ln_mix_g[l]), row(ln_mix_b[l]))
        h = _ffn(h, w_ffn_gate[l].astype(_BF16), w_ffn_up[l].astype(_BF16),
                 w_ffn_down[l].astype(_BF16), row(ln_ffn_g[l]), row(ln_ffn_b[l]),
                 restore_time_order=(l == DEPTH - 1))
    return h.reshape(bsz, seqlen, D_MODEL)
```

```python
import functools

import numpy as np
import jax
import jax.numpy as jnp
from jax import lax
from jax.experimental import pallas as pl
from jax.experimental.pallas import tpu as pltpu

D_MODEL = 1024
ATT_HEADS = 16
ATT_KV_HEADS = 2
ATT_HEAD_DIM = 64
ATT_BLOCK = 128
SSD_D_INNER = 2048
SSD_HEAD_DIM = 64
SSD_HEADS = 32
SSD_GROUPS = 4
SSD_HEADS_PER_GROUP = SSD_HEADS // SSD_GROUPS
SSD_STATE = 128
SSD_CONV = 4
SSD_CHUNK = 128
FFN_HIDDEN = 2816
DEPTH = 2
LN_EPS = 1e-5
RMS_EPS = 1e-5
DEEPNORM_ALPHA = (2 * DEPTH) ** 0.25
LOG2E = 1.4426950408889634

Q_DIM = ATT_HEADS * ATT_HEAD_DIM
KV_DIM = ATT_KV_HEADS * ATT_HEAD_DIM
BC_DIM = SSD_GROUPS * SSD_STATE
CONV_DIM = SSD_D_INNER + 2 * BC_DIM

OFF_Q = 0
OFF_KV = OFF_Q + Q_DIM
OFF_Z = OFF_KV + 2 * KV_DIM
OFF_XBC = OFF_Z + SSD_D_INNER
OFF_DT = OFF_XBC + CONV_DIM
LANES = 128
SUBLANES = 8
FRONT_DIM = OFF_DT + LANES
OFF_GATES = OFF_DT + SSD_HEADS

ALIBI_LANES = 6
SEQ_TILE = 128
TIME_GROUPS = SEQ_TILE // SUBLANES
ROW_TILE = 512
VMEM_LIMIT = 56 * 1024 * 1024

_F32 = jnp.float32
_BF16 = jnp.bfloat16
_NT = (((1,), (1,)), ((), ()))


def _dot(a, b):
    return jnp.dot(a, b, preferred_element_type=_F32)


def _sigmoid(x):
    return 1.0 / (1.0 + jnp.exp2(x * (-LOG2E)))


def _block_time(idx):
    return (idx & (SUBLANES - 1)) * TIME_GROUPS + idx // SUBLANES


def _strided_row_groups(ref, chunk_ref, groups):
    chunks = chunk_ref.shape[0]
    for c in range(chunks):
        chunk_ref[c] = ref[:, c * LANES:(c + 1) * LANES]
    pieces = []
    for start, stride in groups:
        pieces.append(jnp.concatenate(
            [chunk_ref[c, pl.ds(start, SUBLANES, stride=stride), :] for c in range(chunks)], axis=1))
    return jnp.concatenate(pieces, axis=0)


def _load_time_major_to_sublane_major(ref, chunk_ref):
    groups = [(blk * SEQ_TILE + v, TIME_GROUPS)
              for blk in range(ref.shape[0] // SEQ_TILE) for v in range(TIME_GROUPS)]
    return _strided_row_groups(ref, chunk_ref, groups)


def _load_sublane_major_to_time_major(ref, chunk_ref):
    groups = [(blk * SEQ_TILE + (a % 2) * (SEQ_TILE // 2) + a // 2, SUBLANES)
              for blk in range(ref.shape[0] // SEQ_TILE) for a in range(TIME_GROUPS)]
    return _strided_row_groups(ref, chunk_ref, groups)


def _layer_norm(x, g, b):
    mu = jnp.mean(x, axis=-1, keepdims=True)
    xc = x - mu
    var = jnp.mean(xc * xc, axis=-1, keepdims=True)
    return xc * lax.rsqrt(var + LN_EPS) * g + b


def _resident(shape):
    nd = len(shape)
    return pl.BlockSpec(shape, lambda *_: (0,) * nd, pipeline_mode=pl.Buffered(1))


def _alibi_operands():
    def split3(v):
        parts, rest = [], np.asarray(v, np.float64)
        for _ in range(3):
            p = rest.astype(jnp.bfloat16)
            parts.append(p)
            rest = rest - p.astype(np.float64)
        return parts

    idx = np.arange(ATT_BLOCK)
    q_time = (idx % SUBLANES) * TIME_GROUPS + idx // SUBLANES
    key_idx = np.arange(2 * ATT_BLOCK)
    k_idx = key_idx % ATT_BLOCK
    key_time = (k_idx % SUBLANES) * TIME_GROUPS + k_idx // SUBLANES + (key_idx // ATT_BLOCK) * ATT_BLOCK
    q_side = np.zeros((ATT_HEADS // 2, ATT_BLOCK, LANES), jnp.bfloat16)
    k_side = np.zeros((2, 2 * ATT_BLOCK, LANES), jnp.bfloat16)
    for h in range(ATT_HEADS):
        slope = LOG2E * 2.0 ** (-8.0 * (h + 1) / ATT_HEADS)
        lane0 = ALIBI_LANES * (h % 2)
        for t, part in enumerate(split3(np.full(ATT_BLOCK, slope))):
            q_side[h // 2, :, lane0 + t] = part
        for t, part in enumerate(split3(-slope * (q_time + ATT_BLOCK))):
            q_side[h // 2, :, lane0 + 3 + t] = part
    for parity in range(2):
        lane0 = ALIBI_LANES * parity
        k_side[parity, :, lane0:lane0 + 3] = key_time[:, None].astype(jnp.bfloat16)
        k_side[parity, :, lane0 + 3:lane0 + 6] = 1.0
    return jnp.asarray(q_side), jnp.asarray(k_side)


def _attention_phases(q, kv_prev, kv_cur, sinks_ref, alibi_q_ref, alibi_k_ref, first_block, out_list):
    blk = ATT_BLOCK
    half = ATT_HEAD_DIM
    band = jnp.concatenate([kv_prev, kv_cur], axis=0)
    k_tile, v_tile = band[:, :KV_DIM], band[:, KV_DIM:]
    swap = lambda t: jnp.concatenate([t[:, half:], t[:, :half]], axis=1)
    k_swap, v_swap = swap(k_tile), swap(v_tile)
    lane_kv = lax.broadcasted_iota(jnp.int32, (2 * blk, KV_DIM), 1)
    low_kv = lane_kv < half
    zero = jnp.zeros((2 * blk, KV_DIM), _F32)
    k_low = (jnp.where(low_kv, k_tile, zero).astype(_BF16), jnp.where(low_kv, k_swap, zero).astype(_BF16))
    k_high = (jnp.where(low_kv, zero, k_swap).astype(_BF16), jnp.where(low_kv, zero, k_tile).astype(_BF16))
    ones = jnp.ones((2 * blk, KV_DIM), _BF16)
    v_ext = (jnp.concatenate([jnp.where(low_kv, v_tile, v_swap).astype(_BF16), ones], axis=1),
             jnp.concatenate([jnp.where(low_kv, v_swap, v_tile).astype(_BF16), ones], axis=1))

    row = lax.broadcasted_iota(jnp.int32, (blk, 2 * blk), 0)
    col = lax.broadcasted_iota(jnp.int32, (blk, 2 * blk), 1)
    key_time = _block_time(col & (blk - 1)) + (col & blk)
    rel = _block_time(row) + blk - key_time
    first_key = jnp.where(first_block, blk, 0)
    valid = (rel >= 0) & (rel < blk) & (col >= first_key)
    low_out = lax.broadcasted_iota(jnp.int32, (blk, 2 * half), 1) < half
    group = ATT_HEADS // ATT_KV_HEADS
    scores = []
    for h in range(ATT_HEADS):
        j = h // 2
        q_pair = jnp.concatenate([q[:, j * 2 * half:(j + 1) * 2 * half], alibi_q_ref[j]], axis=1)
        keys = jnp.concatenate([(k_low if h % 2 == 0 else k_high)[h // group], alibi_k_ref[h % 2]], axis=1)
        scores.append(lax.dot_general(q_pair, keys, _NT, preferred_element_type=_F32))
    yield
    probs, sink_terms = [], []
    for h in range(ATT_HEADS):
        s = jnp.where(valid, scores[h], -jnp.inf)
        sink = sinks_ref[h] * LOG2E
        m = jnp.maximum(jnp.max(s, axis=-1, keepdims=True), sink)
        probs.append(jnp.exp2(s - m).astype(_BF16))
        sink_terms.append(jnp.exp2(sink - m))
    yield
    outs = []
    for j in range(ATT_HEADS // 2):
        halves = []
        for h in (2 * j, 2 * j + 1):
            res = _dot(probs[h], v_ext[h // group])
            halves.append(res[:, :2 * half] / (res[:, 2 * half:] + sink_terms[h]))
        outs.append(jnp.where(low_out, halves[0], halves[1]))
    out_list.append(jnp.concatenate(outs, axis=-1))


def _ssd_phases(xs, b_mat, c_mat, dt, a_row, s_ref, out_list):
    t = SSD_CHUNK
    hp = SSD_HEAD_DIM
    d_a = dt * a_row
    d_a_hi = d_a.astype(_BF16)
    d_a_lo = (d_a - d_a_hi.astype(_F32)).astype(_BF16)
    r = lax.broadcasted_iota(jnp.int32, (t, t), 0)
    c = lax.broadcasted_iota(jnp.int32, (t, t), 1)
    causal = _block_time(r) >= _block_time(c)
    tri = jnp.where(causal, 1.0, 0.0).astype(_BF16)
    cum2 = (_dot(tri, d_a_hi) + _dot(tri, d_a_lo)) * LOG2E
    c_gs = [c_mat[:, g * SSD_STATE:(g + 1) * SSD_STATE] for g in range(SSD_GROUPS)]
    b_gs = [b_mat[:, g * SSD_STATE:(g + 1) * SSD_STATE] for g in range(SSD_GROUPS)]
    cbs = [lax.dot_general(c_gs[g].astype(_BF16), b_gs[g].astype(_BF16), _NT, preferred_element_type=_F32)
           for g in range(SSD_GROUPS)]
    yield
    cum2_t = cum2.T
    src2_t = cum2_t - (jnp.log(dt) * LOG2E).T
    w_state_t = jnp.exp2(cum2_t[:, t - 1:t] - src2_t)
    chunk_decay = jnp.exp2(cum2[t - 1:t, :])
    lhs, bw = [], []
    for g in range(SSD_GROUPS):
        cb_b = cbs[g].astype(_BF16)
        c_g_b = c_gs[g].astype(_BF16)
        b_g_t_b = b_gs[g].T.astype(_BF16)
        for e in range(SSD_HEADS_PER_GROUP):
            h = g * SSD_HEADS_PER_GROUP + e
            dst2 = jnp.broadcast_to(cum2[:, h:h + 1], (t, t))
            seg2 = jnp.where(causal, dst2 - src2_t[h:h + 1, :], -jnp.inf)
            m_intra = jnp.exp2(seg2).astype(_BF16) * cb_b
            c_scaled = jnp.exp2(dst2).astype(_BF16) * c_g_b
            lhs.append(jnp.concatenate([m_intra, c_scaled], axis=1))
            w_row = jnp.broadcast_to(w_state_t[h:h + 1, :], (SSD_STATE, t)).astype(_BF16)
            bw.append(b_g_t_b * w_row)
    yield
    xs_b = xs.astype(_BF16)
    lane = lax.broadcasted_iota(jnp.int32, (t, 2 * hp), 1)
    low = lane < hp
    low_b = jnp.where(low, 1.0, 0.0).astype(_BF16)
    high_b = jnp.where(low, 0.0, 1.0).astype(_BF16)
    low_row = low[0:1, :]
    ys = []
    for j in range(SSD_HEADS // 2):
        x_pair = xs_b[:, 2 * j * hp:(2 * j + 2) * hp]
        state = s_ref[j]
        rhs = jnp.concatenate([x_pair, state.astype(_BF16)], axis=0)
        ys.append(jnp.where(low, _dot(lhs[2 * j], rhs), _dot(lhs[2 * j + 1], rhs)))
        x_split = jnp.concatenate([x_pair * low_b, x_pair * high_b], axis=0)
        decay = jnp.where(low_row, chunk_decay[:, 2 * j:2 * j + 1], chunk_decay[:, 2 * j + 1:2 * j + 2])
        s_ref[j] = state * decay + _dot(jnp.concatenate([bw[2 * j], bw[2 * j + 1]], axis=1), x_split)
    out_list.append(jnp.concatenate(ys, axis=-1))


def _mixer_front_kernel(*refs, blocks_per_seq, input_ln):
    refs = list(refs)
    sinks_ref, h_ref = refs[:2]
    del refs[:2]
    if input_ln:
        lng_ref, lnb_ref = refs[:2]
        del refs[:2]
    (w_ref, convw_ref, convb_ref, dtb_ref, alog_ref, dskip_ref, normw_ref, alibi_q_ref, alibi_k_ref,
     y_ref, att_ref) = refs[:11]
    del refs[:11]
    if input_ln:
        hout_ref = refs.pop(0)
    s_ref, kv_ref, tail_ref, proj_ref = refs[:4]
    i = pl.program_id(0)
    tm = SEQ_TILE
    tail_rows = (SSD_CONV - 1) * SUBLANES
    starts_sequence = lax.rem(i - 1, blocks_per_seq) == 0

    @pl.when(i == 0)
    def _():
        proj_ref[...] = jnp.zeros_like(proj_ref)

    @pl.when(jnp.logical_or(i == 0, starts_sequence))
    def _():
        s_ref[...] = jnp.zeros_like(s_ref)
        kv_ref[...] = jnp.zeros_like(kv_ref)
        tail_ref[...] = jnp.zeros_like(tail_ref)

    if input_ln:
        h = _layer_norm(_load_time_major_to_sublane_major(h_ref, refs[4]), lng_ref[...], lnb_ref[...])
        hout_ref[...] = h
    else:
        h = h_ref[...]
    hb = h.astype(_BF16)

    def project(lo, hi):
        proj_ref[:, lo:hi] = _dot(hb, w_ref[:, lo:hi])

    q = proj_ref[:, OFF_Q:OFF_KV].astype(_BF16)
    kv_lane = lax.broadcasted_iota(jnp.int32, (1, 2 * KV_DIM), 1)
    kv_cur = proj_ref[:, OFF_KV:OFF_Z] * jnp.where(kv_lane < KV_DIM, LOG2E * ATT_HEAD_DIM ** -0.5, 1.0)
    z = proj_ref[:, OFF_Z:OFF_XBC]
    xbc_pre = proj_ref[:, OFF_XBC:OFF_DT]
    dt_raw = proj_ref[:, OFF_DT:FRONT_DIM]

    att_out, ssd_out = [], []
    att = _attention_phases(q, kv_ref[...], kv_cur, sinks_ref, alibi_q_ref, alibi_k_ref, starts_sequence, att_out)
    next(att)
    kv_ref[...] = kv_cur
    z_gate = z * _sigmoid(z)
    project(OFF_Q, OFF_Z)
    project(OFF_Z, OFF_XBC)

    tail = xbc_pre[tm - tail_rows:, :]
    sublane = lax.broadcasted_iota(jnp.int32, (SUBLANES, CONV_DIM), 0)
    wrapped = []
    for j in range(SSD_CONV - 1):
        rows = slice(j * SUBLANES, (j + 1) * SUBLANES)
        mixed = jnp.where(sublane == SUBLANES - 1, tail_ref[rows, :], tail[rows, :])
        wrapped.append(pltpu.roll(mixed, 1, axis=0))
    ext = jnp.concatenate(wrapped + [xbc_pre], axis=0)
    tail_ref[...] = tail
    acc = convb_ref[...]
    for k in range(SSD_CONV):
        acc = acc + convw_ref[k:k + 1, :] * ext[k * SUBLANES:k * SUBLANES + tm, :]
    xbc = acc * _sigmoid(acc)
    xs = xbc[:, :SSD_D_INNER]
    b_mat = xbc[:, SSD_D_INNER:SSD_D_INNER + BC_DIM]
    c_mat = xbc[:, SSD_D_INNER + BC_DIM:]
    dt_in = dt_raw + dtb_ref[...]
    dt = jnp.maximum(dt_in, 0.0) + jnp.log1p(jnp.exp(-jnp.abs(dt_in)))
    a_row = -jnp.exp(alog_ref[...])
    ssd = _ssd_phases(xs, b_mat, c_mat, dt, a_row, s_ref, ssd_out)
    next(ssd)
    project(OFF_XBC, FRONT_DIM)

    next(att)
    next(ssd)
    for phases in (att, ssd):
        for _ in phases:
            pass
    att_ref[...] = att_out[0].astype(_BF16)
    y = ssd_out[0] + dskip_ref[...] * xs
    y = y * z_gate
    gsz = SSD_D_INNER // SSD_GROUPS
    parts = []
    for g in range(SSD_GROUPS):
        yg = y[:, g * gsz:(g + 1) * gsz]
        ms = jnp.mean(yg * yg, axis=-1, keepdims=True)
        parts.append(yg * lax.rsqrt(ms + RMS_EPS))
    y = jnp.concatenate(parts, axis=-1) * normw_ref[...]
    y_ref[...] = y.astype(_BF16)


def _mixer_front(h2d, seqlen, sinks, w_all, layer, conv_w, conv_b, dt_bias, a_log, d_skip_row, norm_w,
                 input_ln=None):
    n = h2d.shape[0]
    tm = SEQ_TILE
    nblk = n // tm
    alibi_q, alibi_k = _alibi_operands()
    tok_in = pl.BlockSpec((tm, D_MODEL), lambda i: (jnp.minimum(i, nblk - 1), 0))
    tok = lambda width: pl.BlockSpec((tm, width), lambda i: (jnp.maximum(i - 1, 0), 0))
    ln_specs = [_resident((1, D_MODEL)), _resident((1, D_MODEL))] if input_ln else []
    ln_out_specs = [tok_in] if input_ln else []
    ln_out_shapes = [jax.ShapeDtypeStruct((n, D_MODEL), _F32)] if input_ln else []
    ln_scratch = [pltpu.VMEM((D_MODEL // LANES, tm, LANES), _F32)] if input_ln else []
    return pl.pallas_call(
        functools.partial(_mixer_front_kernel, blocks_per_seq=seqlen // tm, input_ln=bool(input_ln)),
        grid=(nblk + 1,),
        in_specs=[pl.BlockSpec(memory_space=pltpu.SMEM), tok_in] + ln_specs + [
                  pl.BlockSpec((None, D_MODEL, FRONT_DIM), lambda i: (layer, 0, 0), pipeline_mode=pl.Buffered(1)),
                  _resident((SSD_CONV, CONV_DIM)), _resident((1, CONV_DIM)),
                  _resident((1, LANES)), _resident((1, LANES)),
                  _resident((1, SSD_D_INNER)), _resident((1, SSD_D_INNER)),
                  _resident(alibi_q.shape), _resident(alibi_k.shape)],
        out_specs=[tok(SSD_D_INNER), tok(Q_DIM)] + ln_out_specs,
        out_shape=[jax.ShapeDtypeStruct((n, SSD_D_INNER), _BF16),
                   jax.ShapeDtypeStruct((n, Q_DIM), _BF16)] + ln_out_shapes,
        scratch_shapes=[pltpu.VMEM((SSD_HEADS // 2, SSD_STATE, 2 * SSD_HEAD_DIM), _F32),
                        pltpu.VMEM((ATT_BLOCK, 2 * KV_DIM), _F32),
                        pltpu.VMEM(((SSD_CONV - 1) * SUBLANES, CONV_DIM), _F32),
                        pltpu.VMEM((tm, FRONT_DIM), _F32)] + ln_scratch,
        compiler_params=pltpu.CompilerParams(dimension_semantics=("arbitrary",),
                                             vmem_limit_bytes=VMEM_LIMIT),
        name="mixer_front",
    )(sinks, h2d, *(input_ln or ()), w_all, conv_w, conv_b, dt_bias, a_log, d_skip_row, norm_w, alibi_q, alibi_k)


def _mixer_back_kernel(h_ref, y_ref, att_ref, wg_ref, wa_ref, wb_ref, wm_ref, g_ref, b_ref, o_ref):
    h = h_ref[...]
    gates = _sigmoid(_dot(h.astype(_BF16), wg_ref[...]))
    y_a = _dot(y_ref[...], wa_ref[...])
    y_b = _dot(att_ref[...], wb_ref[...])
    merged = gates[:, :D_MODEL] * y_a + gates[:, D_MODEL:] * y_b
    mix = _dot(merged.astype(_BF16), wm_ref[...])
    o_ref[...] = _layer_norm(DEEPNORM_ALPHA * h + mix, g_ref[...], b_ref[...])


def _mixer_back(h2d, y2d, att2d, w_gates, w_ssd_out, w_att_out, w_mix_out, ln_g, ln_b):
    n = h2d.shape[0]
    rows = lambda width: pl.BlockSpec((ROW_TILE, width), lambda i: (i, 0))
    return pl.pallas_call(
        _mixer_back_kernel,
        grid=(n // ROW_TILE,),
        in_specs=[rows(D_MODEL), rows(SSD_D_INNER), rows(Q_DIM),
                  _resident((D_MODEL, 2 * D_MODEL)), _resident((SSD_D_INNER, D_MODEL)),
                  _resident((Q_DIM, D_MODEL)), _resident((D_MODEL, D_MODEL)),
                  _resident((1, D_MODEL)), _resident((1, D_MODEL))],
        out_specs=rows(D_MODEL),
        out_shape=jax.ShapeDtypeStruct((n, D_MODEL), _F32),
        compiler_params=pltpu.CompilerParams(dimension_semantics=("arbitrary",),
                                             vmem_limit_bytes=VMEM_LIMIT),
        name="mixer_back",
    )(h2d, y2d, att2d, w_gates, w_ssd_out, w_att_out, w_mix_out, ln_g, ln_b)


def _ffn_kernel(h_ref, wg_ref, wu_ref, wd_ref, g_ref, b_ref, o_ref, *scratch, restore_time_order):
    h = _load_sublane_major_to_time_major(h_ref, scratch[0]) if restore_time_order else h_ref[...]
    hb = h.astype(_BF16)
    gate = _dot(hb, wg_ref[...])
    up = _dot(hb, wu_ref[...])
    act = (gate * _sigmoid(gate) * up).astype(_BF16)
    ffn = _dot(act, wd_ref[...])
    o_ref[...] = _layer_norm(DEEPNORM_ALPHA * h + ffn, g_ref[...], b_ref[...])


def _ffn(h2d, w_gate, w_up, w_down, ln_g, ln_b, restore_time_order):
    n = h2d.shape[0]
    rows = pl.BlockSpec((ROW_TILE, D_MODEL), lambda i: (i, 0))
    return pl.pallas_call(
        functools.partial(_ffn_kernel, restore_time_order=restore_time_order),
        grid=(n // ROW_TILE,),
        in_specs=[rows, _resident((D_MODEL, FFN_HIDDEN)), _resident((D_MODEL, FFN_HIDDEN)),
                  _resident((FFN_HIDDEN, D_MODEL)), _resident((1, D_MODEL)), _resident((1, D_MODEL))],
        out_specs=rows,
        out_shape=jax.ShapeDtypeStruct((n, D_MODEL), _F32),
        scratch_shapes=[pltpu.VMEM((D_MODEL // LANES, ROW_TILE, LANES), _F32)] if restore_time_order else [],
        compiler_params=pltpu.CompilerParams(dimension_semantics=("arbitrary",),
                                             vmem_limit_bytes=VMEM_LIMIT),
        name="ffn",
    )(h2d, w_gate, w_up, w_down, ln_g, ln_b)


def kernel(x, ln_in_g, ln_in_b, w_in, conv_w, conv_b, dt_bias, a_log, d_skip, ssd_norm_w, att_sinks,
           w_ssd_out, w_att_out, w_mix_out, ln_mix_g, ln_mix_b, w_ffn_gate, w_ffn_up, w_ffn_down,
           ln_ffn_g, ln_ffn_b):
    bsz, seqlen, _ = x.shape
    n = bsz * seqlen
    row = lambda v: v.reshape(1, -1)
    pad_heads = lambda v: jnp.pad(v, (0, LANES - SSD_HEADS)).reshape(1, LANES)

    h = x.reshape(n, D_MODEL)
    w_in_b = w_in.astype(_BF16)
    for l in range(DEPTH):
        y, att, *normed = _mixer_front(
            h, seqlen, att_sinks[l], w_in_b, l, conv_w[l], row(conv_b[l]),
            pad_heads(dt_bias[l]), pad_heads(a_log[l]),
            row(jnp.repeat(d_skip[l], SSD_HEAD_DIM)), row(ssd_norm_w[l]),
            input_ln=(row(ln_in_g), row(ln_in_b)) if l == 0 else None)
        if normed:
            h = normed[0]
        h = _mixer_back(h, y, att, w_in_b[l, :, OFF_GATES:],
                        w_ssd_out[l].astype(_BF16), w_att_out[l].astype(_BF16),
                        w_mix_out[l].astype(_BF16), row(ln_mix_g[l]), row(ln_mix_b[l]))
        h = _ffn(h, w_ffn_gate[l].astype(_BF16), w_ffn_up[l].astype(_BF16),
                 w_ffn_down[l].astype(_BF16), row(ln_ffn_g[l]), row(ln_ffn_b[l]),
                 restore_time_order=(l == DEPTH - 1))
    return h.reshape(bsz, seqlen, D_MODEL)
```

```python
import functools

import jax
import jax.numpy as jnp
from jax import lax
from jax.experimental import pallas as pl
from jax.experimental.pallas import tpu as pltpu

D_MODEL = 1024
ATT_HEADS = 16
ATT_KV_HEADS = 2
ATT_HEAD_DIM = 64
ATT_BLOCK = 128
SSD_D_INNER = 2048
SSD_HEAD_DIM = 64
SSD_HEADS = 32
SSD_GROUPS = 4
SSD_HEADS_PER_GROUP = SSD_HEADS // SSD_GROUPS
SSD_STATE = 128
SSD_CONV = 4
SSD_CHUNK = 128
FFN_HIDDEN = 2816
DEPTH = 2
LN_EPS = 1e-5
RMS_EPS = 1e-5
DEEPNORM_ALPHA = (2 * DEPTH) ** 0.25
LOG2E = 1.4426950408889634

Q_DIM = ATT_HEADS * ATT_HEAD_DIM
KV_DIM = ATT_KV_HEADS * ATT_HEAD_DIM
BC_DIM = SSD_GROUPS * SSD_STATE
CONV_DIM = SSD_D_INNER + 2 * BC_DIM

OFF_Q = 0
OFF_KV = OFF_Q + Q_DIM
OFF_Z = OFF_KV + 2 * KV_DIM
OFF_XBC = OFF_Z + SSD_D_INNER
OFF_DT = OFF_XBC + CONV_DIM
LANES = 128
SUBLANES = 8
FRONT_DIM = OFF_DT + LANES
OFF_GATES = OFF_DT + SSD_HEADS

SEQ_TILE = 128
TIME_GROUPS = SEQ_TILE // SUBLANES
ROW_TILE = 512
VMEM_LIMIT = 56 * 1024 * 1024

_F32 = jnp.float32
_BF16 = jnp.bfloat16
_NT = (((1,), (1,)), ((), ()))


def _dot(a, b):
    return jnp.dot(a, b, preferred_element_type=_F32)


def _sigmoid(x):
    return 1.0 / (1.0 + jnp.exp2(x * (-LOG2E)))


def _block_time(idx):
    return (idx & (SUBLANES - 1)) * TIME_GROUPS + idx // SUBLANES


def _strided_row_groups(ref, chunk_ref, groups):
    chunks = chunk_ref.shape[0]
    for c in range(chunks):
        chunk_ref[c] = ref[:, c * LANES:(c + 1) * LANES]
    pieces = []
    for start, stride in groups:
        pieces.append(jnp.concatenate(
            [chunk_ref[c, pl.ds(start, SUBLANES, stride=stride), :] for c in range(chunks)], axis=1))
    return jnp.concatenate(pieces, axis=0)


def _load_time_major_to_sublane_major(ref, chunk_ref):
    groups = [(blk * SEQ_TILE + v, TIME_GROUPS)
              for blk in range(ref.shape[0] // SEQ_TILE) for v in range(TIME_GROUPS)]
    return _strided_row_groups(ref, chunk_ref, groups)


def _load_sublane_major_to_time_major(ref, chunk_ref):
    groups = [(blk * SEQ_TILE + (a % 2) * (SEQ_TILE // 2) + a // 2, SUBLANES)
              for blk in range(ref.shape[0] // SEQ_TILE) for a in range(TIME_GROUPS)]
    return _strided_row_groups(ref, chunk_ref, groups)


def _layer_norm(x, g, b):
    mu = jnp.mean(x, axis=-1, keepdims=True)
    xc = x - mu
    var = jnp.mean(xc * xc, axis=-1, keepdims=True)
    return xc * lax.rsqrt(var + LN_EPS) * g + b


def _resident(shape):
    nd = len(shape)
    return pl.BlockSpec(shape, lambda *_: (0,) * nd, pipeline_mode=pl.Buffered(1))


def _attention_phases(q, kv_prev, kv_cur, sinks_ref, first_block, out_list):
    blk = ATT_BLOCK
    half = ATT_HEAD_DIM
    band = jnp.concatenate([kv_prev, kv_cur], axis=0)
    k_tile, v_tile = band[:, :KV_DIM], band[:, KV_DIM:]
    swap = lambda t: jnp.concatenate([t[:, half:], t[:, :half]], axis=1)
    k_swap, v_swap = swap(k_tile), swap(v_tile)
    lane_kv = lax.broadcasted_iota(jnp.int32, (2 * blk, KV_DIM), 1)
    low_kv = lane_kv < half
    zero = jnp.zeros((2 * blk, KV_DIM), _F32)
    k_low = (jnp.where(low_kv, k_tile, zero).astype(_BF16), jnp.where(low_kv, k_swap, zero).astype(_BF16))
    k_high = (jnp.where(low_kv, zero, k_swap).astype(_BF16), jnp.where(low_kv, zero, k_tile).astype(_BF16))
    ones = jnp.ones((2 * blk, KV_DIM), _BF16)
    v_ext = (jnp.concatenate([jnp.where(low_kv, v_tile, v_swap).astype(_BF16), ones], axis=1),
             jnp.concatenate([jnp.where(low_kv, v_swap, v_tile).astype(_BF16), ones], axis=1))

    row = lax.broadcasted_iota(jnp.int32, (blk, 2 * blk), 0)
    col = lax.broadcasted_iota(jnp.int32, (blk, 2 * blk), 1)
    key_time = _block_time(col & (blk - 1)) + (col & blk)
    rel = _block_time(row) + blk - key_time
    first_key = jnp.where(first_block, blk, 0)
    valid = (rel >= 0) & (rel < blk) & (col >= first_key)
    rel_f = rel.astype(_F32)
    low_out = lax.broadcasted_iota(jnp.int32, (blk, 2 * half), 1) < half
    group = ATT_HEADS // ATT_KV_HEADS
    scores = []
    for h in range(ATT_HEADS):
        q_pair = q[:, (h // 2) * 2 * half:(h // 2 + 1) * 2 * half]
        keys = (k_low if h % 2 == 0 else k_high)[h // group]
        scores.append(lax.dot_general(q_pair, keys, _NT, preferred_element_type=_F32))
    yield
    probs, sink_terms = [], []
    for h in range(ATT_HEADS):
        slope = LOG2E * 2.0 ** (-8.0 * (h + 1) / ATT_HEADS)
        s = jnp.where(valid, scores[h] - slope * rel_f, -jnp.inf)
        sink = sinks_ref[h] * LOG2E
        m = jnp.maximum(jnp.max(s, axis=-1, keepdims=True), sink)
        probs.append(jnp.exp2(s - m).astype(_BF16))
        sink_terms.append(jnp.exp2(sink - m))
    yield
    outs = []
    for j in range(ATT_HEADS // 2):
        halves = []
        for h in (2 * j, 2 * j + 1):
            res = _dot(probs[h], v_ext[h // group])
            halves.append(res[:, :2 * half] / (res[:, 2 * half:] + sink_terms[h]))
        outs.append(jnp.where(low_out, halves[0], halves[1]))
    out_list.append(jnp.concatenate(outs, axis=-1))


def _ssd_phases(xs, b_mat, c_mat, dt, a_row, s_ref, out_list):
    t = SSD_CHUNK
    hp = SSD_HEAD_DIM
    d_a = dt * a_row
    d_a_hi = d_a.astype(_BF16)
    d_a_lo = (d_a - d_a_hi.astype(_F32)).astype(_BF16)
    r = lax.broadcasted_iota(jnp.int32, (t, t), 0)
    c = lax.broadcasted_iota(jnp.int32, (t, t), 1)
    causal = _block_time(r) >= _block_time(c)
    tri = jnp.where(causal, 1.0, 0.0).astype(_BF16)
    cum2 = (_dot(tri, d_a_hi) + _dot(tri, d_a_lo)) * LOG2E
    c_gs = [c_mat[:, g * SSD_STATE:(g + 1) * SSD_STATE] for g in range(SSD_GROUPS)]
    b_gs = [b_mat[:, g * SSD_STATE:(g + 1) * SSD_STATE] for g in range(SSD_GROUPS)]
    cbs = [lax.dot_general(c_gs[g].astype(_BF16), b_gs[g].astype(_BF16), _NT, preferred_element_type=_F32)
           for g in range(SSD_GROUPS)]
    yield
    cum2_t = cum2.T
    src2_t = cum2_t - (jnp.log(dt) * LOG2E).T
    w_state_t = jnp.exp2(cum2_t[:, t - 1:t] - src2_t)
    chunk_decay = jnp.exp2(cum2[t - 1:t, :])
    lhs, bw = [], []
    for g in range(SSD_GROUPS):
        b_g_t = b_gs[g].T
        for e in range(SSD_HEADS_PER_GROUP):
            h = g * SSD_HEADS_PER_GROUP + e
            dst2 = jnp.broadcast_to(cum2[:, h:h + 1], (t, t))
            seg2 = jnp.where(causal, dst2 - src2_t[h:h + 1, :], -jnp.inf)
            m_intra = jnp.exp2(seg2) * cbs[g]
            c_scaled = c_gs[g] * jnp.exp2(dst2)
            lhs.append(jnp.concatenate([m_intra, c_scaled], axis=1).astype(_BF16))
            bw.append((b_g_t * w_state_t[h:h + 1, :]).astype(_BF16))
    yield
    xs_b = xs.astype(_BF16)
    lane = lax.broadcasted_iota(jnp.int32, (t, 2 * hp), 1)
    low = lane < hp
    low_b = jnp.where(low, 1.0, 0.0).astype(_BF16)
    high_b = jnp.where(low, 0.0, 1.0).astype(_BF16)
    low_row = low[0:1, :]
    ys = []
    for j in range(SSD_HEADS // 2):
        x_pair = xs_b[:, 2 * j * hp:(2 * j + 2) * hp]
        state = s_ref[j]
        rhs = jnp.concatenate([x_pair, state.astype(_BF16)], axis=0)
        ys.append(jnp.where(low, _dot(lhs[2 * j], rhs), _dot(lhs[2 * j + 1], rhs)))
        x_split = jnp.concatenate([x_pair * low_b, x_pair * high_b], axis=0)
        decay = jnp.where(low_row, chunk_decay[:, 2 * j:2 * j + 1], chunk_decay[:, 2 * j + 1:2 * j + 2])
        s_ref[j] = state * decay + _dot(jnp.concatenate([bw[2 * j], bw[2 * j + 1]], axis=1), x_split)
    out_list.append(jnp.concatenate(ys, axis=-1))


def _mixer_front_kernel(*refs, blocks_per_seq, input_ln):
    refs = list(refs)
    sinks_ref, h_ref = refs[:2]
    del refs[:2]
    if input_ln:
        lng_ref, lnb_ref = refs[:2]
        del refs[:2]
    w_ref, convw_ref, convb_ref, dtb_ref, alog_ref, dskip_ref, normw_ref, y_ref, att_ref = refs[:9]
    del refs[:9]
    if input_ln:
        hout_ref = refs.pop(0)
    s_ref, kv_ref, tail_ref, proj_ref = refs[:4]
    i = pl.program_id(0)
    tm = SEQ_TILE
    tail_rows = (SSD_CONV - 1) * SUBLANES
    starts_sequence = lax.rem(i - 1, blocks_per_seq) == 0

    @pl.when(i == 0)
    def _():
        proj_ref[...] = jnp.zeros_like(proj_ref)

    @pl.when(jnp.logical_or(i == 0, starts_sequence))
    def _():
        s_ref[...] = jnp.zeros_like(s_ref)
        kv_ref[...] = jnp.zeros_like(kv_ref)
        tail_ref[...] = jnp.zeros_like(tail_ref)

    if input_ln:
        h = _layer_norm(_load_time_major_to_sublane_major(h_ref, refs[4]), lng_ref[...], lnb_ref[...])
        hout_ref[...] = h
    else:
        h = h_ref[...]
    hb = h.astype(_BF16)

    def project(lo, hi):
        proj_ref[:, lo:hi] = _dot(hb, w_ref[:, lo:hi])

    q = proj_ref[:, OFF_Q:OFF_KV].astype(_BF16)
    kv_lane = lax.broadcasted_iota(jnp.int32, (1, 2 * KV_DIM), 1)
    kv_cur = proj_ref[:, OFF_KV:OFF_Z] * jnp.where(kv_lane < KV_DIM, LOG2E * ATT_HEAD_DIM ** -0.5, 1.0)
    z = proj_ref[:, OFF_Z:OFF_XBC]
    xbc_pre = proj_ref[:, OFF_XBC:OFF_DT]
    dt_raw = proj_ref[:, OFF_DT:FRONT_DIM]

    att_out, ssd_out = [], []
    att = _attention_phases(q, kv_ref[...], kv_cur, sinks_ref, starts_sequence, att_out)
    next(att)
    kv_ref[...] = kv_cur
    z_gate = z * _sigmoid(z)
    project(OFF_Q, OFF_Z)

    tail = xbc_pre[tm - tail_rows:, :]
    sublane = lax.broadcasted_iota(jnp.int32, (SUBLANES, CONV_DIM), 0)
    wrapped = []
    for j in range(SSD_CONV - 1):
        rows = slice(j * SUBLANES, (j + 1) * SUBLANES)
        mixed = jnp.where(sublane == SUBLANES - 1, tail_ref[rows, :], tail[rows, :])
        wrapped.append(pltpu.roll(mixed, 1, axis=0))
    ext = jnp.concatenate(wrapped + [xbc_pre], axis=0)
    tail_ref[...] = tail
    acc = convb_ref[...]
    for k in range(SSD_CONV):
        acc = acc + convw_ref[k:k + 1, :] * ext[k * SUBLANES:k * SUBLANES + tm, :]
    xbc = acc * _sigmoid(acc)
    xs = xbc[:, :SSD_D_INNER]
    b_mat = xbc[:, SSD_D_INNER:SSD_D_INNER + BC_DIM]
    c_mat = xbc[:, SSD_D_INNER + BC_DIM:]
    dt_in = dt_raw + dtb_ref[...]
    dt = jnp.maximum(dt_in, 0.0) + jnp.log1p(jnp.exp(-jnp.abs(dt_in)))
    a_row = -jnp.exp(alog_ref[...])
    ssd = _ssd_phases(xs, b_mat, c_mat, dt, a_row, s_ref, ssd_out)
    next(ssd)
    project(OFF_Z, OFF_XBC)
    project(OFF_XBC, FRONT_DIM)

    next(att)
    next(ssd)
    for phases in (att, ssd):
        for _ in phases:
            pass
    att_ref[...] = att_out[0].astype(_BF16)
    y = ssd_out[0] + dskip_ref[...] * xs
    y = y * z_gate
    gsz = SSD_D_INNER // SSD_GROUPS
    parts = []
    for g in range(SSD_GROUPS):
        yg = y[:, g * gsz:(g + 1) * gsz]
        ms = jnp.mean(yg * yg, axis=-1, keepdims=True)
        parts.append(yg * lax.rsqrt(ms + RMS_EPS))
    y = jnp.concatenate(parts, axis=-1) * normw_ref[...]
    y_ref[...] = y.astype(_BF16)


def _mixer_front(h2d, seqlen, sinks, w_all, layer, conv_w, conv_b, dt_bias, a_log, d_skip_row, norm_w,
                 input_ln=None):
    n = h2d.shape[0]
    tm = SEQ_TILE
    nblk = n // tm
    tok_in = pl.BlockSpec((tm, D_MODEL), lambda i: (jnp.minimum(i, nblk - 1), 0))
    tok = lambda width: pl.BlockSpec((tm, width), lambda i: (jnp.maximum(i - 1, 0), 0))
    ln_specs = [_resident((1, D_MODEL)), _resident((1, D_MODEL))] if input_ln else []
    ln_out_specs = [tok_in] if input_ln else []
    ln_out_shapes = [jax.ShapeDtypeStruct((n, D_MODEL), _F32)] if input_ln else []
    ln_scratch = [pltpu.VMEM((D_MODEL // LANES, tm, LANES), _F32)] if input_ln else []
    return pl.pallas_call(
        functools.partial(_mixer_front_kernel, blocks_per_seq=seqlen // tm, input_ln=bool(input_ln)),
        grid=(nblk + 1,),
        in_specs=[pl.BlockSpec(memory_space=pltpu.SMEM), tok_in] + ln_specs + [
                  pl.BlockSpec((None, D_MODEL, FRONT_DIM), lambda i: (layer, 0, 0), pipeline_mode=pl.Buffered(1)),
                  _resident((SSD_CONV, CONV_DIM)), _resident((1, CONV_DIM)),
                  _resident((1, LANES)), _resident((1, LANES)),
                  _resident((1, SSD_D_INNER)), _resident((1, SSD_D_INNER))],
        out_specs=[tok(SSD_D_INNER), tok(Q_DIM)] + ln_out_specs,
        out_shape=[jax.ShapeDtypeStruct((n, SSD_D_INNER), _BF16),
                   jax.ShapeDtypeStruct((n, Q_DIM), _BF16)] + ln_out_shapes,
        scratch_shapes=[pltpu.VMEM((SSD_HEADS // 2, SSD_STATE, 2 * SSD_HEAD_DIM), _F32),
                        pltpu.VMEM((ATT_BLOCK, 2 * KV_DIM), _F32),
                        pltpu.VMEM(((SSD_CONV - 1) * SUBLANES, CONV_DIM), _F32),
                        pltpu.VMEM((tm, FRONT_DIM), _F32)] + ln_scratch,
        compiler_params=pltpu.CompilerParams(dimension_semantics=("arbitrary",),
                                             vmem_limit_bytes=VMEM_LIMIT),
        name="mixer_front",
    )(sinks, h2d, *(input_ln or ()), w_all, conv_w, conv_b, dt_bias, a_log, d_skip_row, norm_w)


def _mixer_back_kernel(h_ref, y_ref, att_ref, wg_ref, wa_ref, wb_ref, wm_ref, g_ref, b_ref, o_ref):
    h = h_ref[...]
    gates = _sigmoid(_dot(h.astype(_BF16), wg_ref[...]))
    y_a = _dot(y_ref[...], wa_ref[...])
    y_b = _dot(att_ref[...], wb_ref[...])
    merged = gates[:, :D_MODEL] * y_a + gates[:, D_MODEL:] * y_b
    mix = _dot(merged.astype(_BF16), wm_ref[...])
    o_ref[...] = _layer_norm(DEEPNORM_ALPHA * h + mix, g_ref[...], b_ref[...])


def _mixer_back(h2d, y2d, att2d, w_gates, w_ssd_out, w_att_out, w_mix_out, ln_g, ln_b):
    n = h2d.shape[0]
    rows = lambda width: pl.BlockSpec((ROW_TILE, width), lambda i: (i, 0))
    return pl.pallas_call(
        _mixer_back_kernel,
        grid=(n // ROW_TILE,),
        in_specs=[rows(D_MODEL), rows(SSD_D_INNER), rows(Q_DIM),
                  _resident((D_MODEL, 2 * D_MODEL)), _resident((SSD_D_INNER, D_MODEL)),
                  _resident((Q_DIM, D_MODEL)), _resident((D_MODEL, D_MODEL)),
                  _resident((1, D_MODEL)), _resident((1, D_MODEL))],
        out_specs=rows(D_MODEL),
        out_shape=jax.ShapeDtypeStruct((n, D_MODEL), _F32),
        compiler_params=pltpu.CompilerParams(dimension_semantics=("arbitrary",),
                                             vmem_limit_bytes=VMEM_LIMIT),
        name="mixer_back",
    )(h2d, y2d, att2d, w_gates, w_ssd_out, w_att_out, w_mix_out, ln_g, ln_b)


def _ffn_kernel(h_ref, wg_ref, wu_ref, wd_ref, g_ref, b_ref, o_ref, *scratch, restore_time_order):
    h = _load_sublane_major_to_time_major(h_ref, scratch[0]) if restore_time_order else h_ref[...]
    hb = h.astype(_BF16)
    gate = _dot(hb, wg_ref[...])
    up = _dot(hb, wu_ref[...])
    act = (gate * _sigmoid(gate) * up).astype(_BF16)
    ffn = _dot(act, wd_ref[...])
    o_ref[...] = _layer_norm(DEEPNORM_ALPHA * h + ffn, g_ref[...], b_ref[...])


def _ffn(h2d, w_gate, w_up, w_down, ln_g, ln_b, restore_time_order):
    n = h2d.shape[0]
    rows = pl.BlockSpec((ROW_TILE, D_MODEL), lambda i: (i, 0))
    return pl.pallas_call(
        functools.partial(_ffn_kernel, restore_time_order=restore_time_order),
        grid=(n // ROW_TILE,),
        in_specs=[rows, _resident((D_MODEL, FFN_HIDDEN)), _resident((D_MODEL, FFN_HIDDEN)),
                  _resident((FFN_HIDDEN, D_MODEL)), _resident((1, D_MODEL)), _resident((1, D_MODEL))],
        out_specs=rows,
        out_shape=jax.ShapeDtypeStruct((n, D_MODEL), _F32),
        scratch_shapes=[pltpu.VMEM((D_MODEL // LANES, ROW_TILE, LANES), _F32)] if restore_time_order else [],
        compiler_params=pltpu.CompilerParams(dimension_semantics=("arbitrary",),
                                             vmem_limit_bytes=VMEM_LIMIT),
        name="ffn",
    )(h2d, w_gate, w_up, w_down, ln_g, ln_b)


def kernel(x, ln_in_g, ln_in_b, w_in, conv_w, conv_b, dt_bias, a_log, d_skip, ssd_norm_w, att_sinks,
           w_ssd_out, w_att_out, w_mix_out, ln_mix_g, ln_mix_b, w_ffn_gate, w_ffn_up, w_ffn_down,
           ln_ffn_g, ln_ffn_b):
    bsz, seqlen, _ = x.shape
    n = bsz * seqlen
    row = lambda v: v.reshape(1, -1)
    pad_heads = lambda v: jnp.pad(v, (0, LANES - SSD_HEADS)).reshape(1, LANES)

    h = x.reshape(n, D_MODEL)
    w_in_b = w_in.astype(_BF16)
    for l in range(DEPTH):
        y, att, *normed = _mixer_front(
            h, seqlen, att_sinks[l], w_in_b, l, conv_w[l], row(conv_b[l]),
            pad_heads(dt_bias[l]), pad_heads(a_log[l]),
            row(jnp.repeat(d_skip[l], SSD_HEAD_DIM)), row(ssd_norm_w[l]),
            input_ln=(row(ln_in_g), row(ln_in_b)) if l == 0 else None)
        if normed:
            h = normed[0]
        h = _mixer_back(h, y, att, w_in_b[l, :, OFF_GATES:],
                        w_ssd_out[l].astype(_BF16), w_att_out[l].astype(_BF16),
                        w_mix_out[l].astype(_BF16), row(ln_mix_g[l]), row(ln_mix_b[l]))
        h = _ffn(h, w_ffn_gate[l].astype(_BF16), w_ffn_up[l].astype(_BF16),
                 w_ffn_down[l].astype(_BF16), row(ln_ffn_g[l]), row(ln_ffn_b[l]),
                 restore_time_order=(l == DEPTH - 1))
    return h.reshape(bsz, seqlen, D_MODEL)
```

```python
import functools

import jax
import jax.numpy as jnp
from jax import lax
from jax.experimental import pallas as pl
from jax.experimental.pallas import tpu as pltpu

D_MODEL = 1024
ATT_HEADS = 16
ATT_KV_HEADS = 2
ATT_HEAD_DIM = 64
ATT_BLOCK = 128
SSD_D_INNER = 2048
SSD_HEAD_DIM = 64
SSD_HEADS = 32
SSD_GROUPS = 4
SSD_HEADS_PER_GROUP = SSD_HEADS // SSD_GROUPS
SSD_STATE = 128
SSD_CONV = 4
SSD_CHUNK = 128
FFN_HIDDEN = 2816
DEPTH = 2
LN_EPS = 1e-5
RMS_EPS = 1e-5
DEEPNORM_ALPHA = (2 * DEPTH) ** 0.25
LOG2E = 1.4426950408889634

Q_DIM = ATT_HEADS * ATT_HEAD_DIM
KV_DIM = ATT_KV_HEADS * ATT_HEAD_DIM
BC_DIM = SSD_GROUPS * SSD_STATE
CONV_DIM = SSD_D_INNER + 2 * BC_DIM

OFF_Q = 0
OFF_KV = OFF_Q + Q_DIM
OFF_Z = OFF_KV + 2 * KV_DIM
OFF_XBC = OFF_Z + SSD_D_INNER
OFF_DT = OFF_XBC + CONV_DIM
LANES = 128
SUBLANES = 8
FRONT_DIM = OFF_DT + LANES
OFF_GATES = OFF_DT + SSD_HEADS

SEQ_TILE = 128
TIME_GROUPS = SEQ_TILE // SUBLANES
ROW_TILE = 512
VMEM_LIMIT = 56 * 1024 * 1024

_F32 = jnp.float32
_BF16 = jnp.bfloat16
_NT = (((1,), (1,)), ((), ()))


def _dot(a, b):
    return jnp.dot(a, b, preferred_element_type=_F32)


def _sigmoid(x):
    return 1.0 / (1.0 + jnp.exp2(x * (-LOG2E)))


def _block_time(idx):
    return (idx & (SUBLANES - 1)) * TIME_GROUPS + idx // SUBLANES


def _strided_row_groups(ref, chunk_ref, groups):
    chunks = chunk_ref.shape[0]
    for c in range(chunks):
        chunk_ref[c] = ref[:, c * LANES:(c + 1) * LANES]
    pieces = []
    for start, stride in groups:
        pieces.append(jnp.concatenate(
            [chunk_ref[c, pl.ds(start, SUBLANES, stride=stride), :] for c in range(chunks)], axis=1))
    return jnp.concatenate(pieces, axis=0)


def _load_time_major_to_sublane_major(ref, chunk_ref):
    groups = [(blk * SEQ_TILE + v, TIME_GROUPS)
              for blk in range(ref.shape[0] // SEQ_TILE) for v in range(TIME_GROUPS)]
    return _strided_row_groups(ref, chunk_ref, groups)


def _load_sublane_major_to_time_major(ref, chunk_ref):
    groups = [(blk * SEQ_TILE + (a % 2) * (SEQ_TILE // 2) + a // 2, SUBLANES)
              for blk in range(ref.shape[0] // SEQ_TILE) for a in range(TIME_GROUPS)]
    return _strided_row_groups(ref, chunk_ref, groups)


def _layer_norm(x, g, b):
    mu = jnp.mean(x, axis=-1, keepdims=True)
    xc = x - mu
    var = jnp.mean(xc * xc, axis=-1, keepdims=True)
    return xc * lax.rsqrt(var + LN_EPS) * g + b


def _resident(shape):
    nd = len(shape)
    return pl.BlockSpec(shape, lambda *_: (0,) * nd, pipeline_mode=pl.Buffered(1))


def _attention_phases(q, kv_prev, kv_cur, sinks_ref, first_block, out_list):
    blk = ATT_BLOCK
    half = ATT_HEAD_DIM
    band = jnp.concatenate([kv_prev, kv_cur], axis=0)
    k_tile, v_tile = band[:, :KV_DIM], band[:, KV_DIM:]
    swap = lambda t: jnp.concatenate([t[:, half:], t[:, :half]], axis=1)
    k_swap, v_swap = swap(k_tile), swap(v_tile)
    lane_kv = lax.broadcasted_iota(jnp.int32, (2 * blk, KV_DIM), 1)
    low_kv = lane_kv < half
    zero = jnp.zeros((2 * blk, KV_DIM), _F32)
    k_low = (jnp.where(low_kv, k_tile, zero).astype(_BF16), jnp.where(low_kv, k_swap, zero).astype(_BF16))
    k_high = (jnp.where(low_kv, zero, k_swap).astype(_BF16), jnp.where(low_kv, zero, k_tile).astype(_BF16))
    ones = jnp.ones((2 * blk, KV_DIM), _BF16)
    v_ext = (jnp.concatenate([jnp.where(low_kv, v_tile, v_swap).astype(_BF16), ones], axis=1),
             jnp.concatenate([jnp.where(low_kv, v_swap, v_tile).astype(_BF16), ones], axis=1))

    row = lax.broadcasted_iota(jnp.int32, (blk, 2 * blk), 0)
    col = lax.broadcasted_iota(jnp.int32, (blk, 2 * blk), 1)
    key_time = _block_time(col & (blk - 1)) + (col & blk)
    rel = _block_time(row) + blk - key_time
    first_key = jnp.where(first_block, blk, 0)
    valid = (rel >= 0) & (rel < blk) & (col >= first_key)
    rel_f = rel.astype(_F32)
    low_out = lax.broadcasted_iota(jnp.int32, (blk, 2 * half), 1) < half
    group = ATT_HEADS // ATT_KV_HEADS
    scores = []
    for h in range(ATT_HEADS):
        q_pair = q[:, (h // 2) * 2 * half:(h // 2 + 1) * 2 * half]
        keys = (k_low if h % 2 == 0 else k_high)[h // group]
        scores.append(lax.dot_general(q_pair, keys, _NT, preferred_element_type=_F32))
    yield
    probs, sink_terms = [], []
    for h in range(ATT_HEADS):
        slope = LOG2E * 2.0 ** (-8.0 * (h + 1) / ATT_HEADS)
        s = jnp.where(valid, scores[h] - slope * rel_f, -jnp.inf)
        sink = sinks_ref[h] * LOG2E
        m = jnp.maximum(jnp.max(s, axis=-1, keepdims=True), sink)
        probs.append(jnp.exp2(s - m).astype(_BF16))
        sink_terms.append(jnp.exp2(sink - m))
    yield
    outs = []
    for j in range(ATT_HEADS // 2):
        halves = []
        for h in (2 * j, 2 * j + 1):
            res = _dot(probs[h], v_ext[h // group])
            halves.append(res[:, :2 * half] / (res[:, 2 * half:] + sink_terms[h]))
        outs.append(jnp.where(low_out, halves[0], halves[1]))
    out_list.append(jnp.concatenate(outs, axis=-1))


def _ssd_phases(xs, b_mat, c_mat, dt, a_row, s_ref, out_list):
    t = SSD_CHUNK
    hp = SSD_HEAD_DIM
    d_a = dt * a_row
    d_a_hi = d_a.astype(_BF16)
    d_a_lo = (d_a - d_a_hi.astype(_F32)).astype(_BF16)
    r = lax.broadcasted_iota(jnp.int32, (t, t), 0)
    c = lax.broadcasted_iota(jnp.int32, (t, t), 1)
    causal = _block_time(r) >= _block_time(c)
    tri = jnp.where(causal, 1.0, 0.0).astype(_BF16)
    cum2 = (_dot(tri, d_a_hi) + _dot(tri, d_a_lo)) * LOG2E
    c_gs = [c_mat[:, g * SSD_STATE:(g + 1) * SSD_STATE] for g in range(SSD_GROUPS)]
    b_gs = [b_mat[:, g * SSD_STATE:(g + 1) * SSD_STATE] for g in range(SSD_GROUPS)]
    cbs = [lax.dot_general(c_gs[g].astype(_BF16), b_gs[g].astype(_BF16), _NT, preferred_element_type=_F32)
           for g in range(SSD_GROUPS)]
    yield
    cum2_t = cum2.T
    src2_t = cum2_t - (jnp.log(dt) * LOG2E).T
    w_state_t = jnp.exp2(cum2_t[:, t - 1:t] - src2_t)
    chunk_decay = jnp.exp2(cum2[t - 1:t, :])
    lhs, bw = [], []
    for g in range(SSD_GROUPS):
        b_g_t = b_gs[g].T
        for e in range(SSD_HEADS_PER_GROUP):
            h = g * SSD_HEADS_PER_GROUP + e
            dst2 = jnp.broadcast_to(cum2[:, h:h + 1], (t, t))
            seg2 = jnp.where(causal, dst2 - src2_t[h:h + 1, :], -jnp.inf)
            m_intra = jnp.exp2(seg2) * cbs[g]
            c_scaled = c_gs[g] * jnp.exp2(dst2)
            lhs.append(jnp.concatenate([m_intra, c_scaled], axis=1).astype(_BF16))
            bw.append((b_g_t * w_state_t[h:h + 1, :]).astype(_BF16))
    yield
    xs_b = xs.astype(_BF16)
    lane = lax.broadcasted_iota(jnp.int32, (t, 2 * hp), 1)
    low = lane < hp
    low_b = jnp.where(low, 1.0, 0.0).astype(_BF16)
    high_b = jnp.where(low, 0.0, 1.0).astype(_BF16)
    low_row = low[0:1, :]
    ys = []
    for j in range(SSD_HEADS // 2):
        x_pair = xs_b[:, 2 * j * hp:(2 * j + 2) * hp]
        state = s_ref[j]
        rhs = jnp.concatenate([x_pair, state.astype(_BF16)], axis=0)
        ys.append(jnp.where(low, _dot(lhs[2 * j], rhs), _dot(lhs[2 * j + 1], rhs)))
        x_split = jnp.concatenate([x_pair * low_b, x_pair * high_b], axis=0)
        decay = jnp.where(low_row, chunk_decay[:, 2 * j:2 * j + 1], chunk_decay[:, 2 * j + 1:2 * j + 2])
        s_ref[j] = state * decay + _dot(jnp.concatenate([bw[2 * j], bw[2 * j + 1]], axis=1), x_split)
    out_list.append(jnp.concatenate(ys, axis=-1))


def _mixer_front_kernel(*refs, blocks_per_seq, input_ln):
    refs = list(refs)
    sinks_ref, h_ref = refs[:2]
    del refs[:2]
    if input_ln:
        lng_ref, lnb_ref = refs[:2]
        del refs[:2]
    w_ref, convw_ref, convb_ref, dtb_ref, alog_ref, dskip_ref, y_ref, att_ref, z_ref = refs[:9]
    del refs[:9]
    if input_ln:
        hout_ref = refs.pop(0)
    s_ref, kv_ref, tail_ref, proj_ref = refs[:4]
    i = pl.program_id(0)
    tm = SEQ_TILE
    tail_rows = (SSD_CONV - 1) * SUBLANES
    starts_sequence = lax.rem(i - 1, blocks_per_seq) == 0

    @pl.when(i == 0)
    def _():
        proj_ref[...] = jnp.zeros_like(proj_ref)

    @pl.when(jnp.logical_or(i == 0, starts_sequence))
    def _():
        s_ref[...] = jnp.zeros_like(s_ref)
        kv_ref[...] = jnp.zeros_like(kv_ref)
        tail_ref[...] = jnp.zeros_like(tail_ref)

    if input_ln:
        h = _layer_norm(_load_time_major_to_sublane_major(h_ref, refs[4]), lng_ref[...], lnb_ref[...])
        hout_ref[...] = h
    else:
        h = h_ref[...]
    hb = h.astype(_BF16)

    def project(lo, hi):
        proj_ref[:, lo:hi] = _dot(hb, w_ref[:, lo:hi])

    q = proj_ref[:, OFF_Q:OFF_KV].astype(_BF16)
    kv_lane = lax.broadcasted_iota(jnp.int32, (1, 2 * KV_DIM), 1)
    kv_cur = proj_ref[:, OFF_KV:OFF_Z] * jnp.where(kv_lane < KV_DIM, LOG2E * ATT_HEAD_DIM ** -0.5, 1.0)
    xbc_pre = proj_ref[:, OFF_XBC:OFF_DT]
    dt_raw = proj_ref[:, OFF_DT:FRONT_DIM]

    att_out, ssd_out = [], []
    att = _attention_phases(q, kv_ref[...], kv_cur, sinks_ref, starts_sequence, att_out)
    next(att)
    kv_ref[...] = kv_cur
    project(OFF_Q, OFF_Z)

    tail = xbc_pre[tm - tail_rows:, :]
    sublane = lax.broadcasted_iota(jnp.int32, (SUBLANES, CONV_DIM), 0)
    wrapped = []
    for j in range(SSD_CONV - 1):
        rows = slice(j * SUBLANES, (j + 1) * SUBLANES)
        mixed = jnp.where(sublane == SUBLANES - 1, tail_ref[rows, :], tail[rows, :])
        wrapped.append(pltpu.roll(mixed, 1, axis=0))
    ext = jnp.concatenate(wrapped + [xbc_pre], axis=0)
    tail_ref[...] = tail
    acc = convb_ref[...]
    for k in range(SSD_CONV):
        acc = acc + convw_ref[k:k + 1, :] * ext[k * SUBLANES:k * SUBLANES + tm, :]
    xbc = acc * _sigmoid(acc)
    xs = xbc[:, :SSD_D_INNER]
    b_mat = xbc[:, SSD_D_INNER:SSD_D_INNER + BC_DIM]
    c_mat = xbc[:, SSD_D_INNER + BC_DIM:]
    dt_in = dt_raw + dtb_ref[...]
    dt = jnp.maximum(dt_in, 0.0) + jnp.log1p(jnp.exp(-jnp.abs(dt_in)))
    a_row = -jnp.exp(alog_ref[...])
    ssd = _ssd_phases(xs, b_mat, c_mat, dt, a_row, s_ref, ssd_out)
    next(ssd)
    z_ref[...] = _dot(hb, w_ref[:, OFF_Z:OFF_XBC])
    project(OFF_XBC, FRONT_DIM)

    next(att)
    next(ssd)
    for phases in (att, ssd):
        for _ in phases:
            pass
    att_ref[...] = att_out[0].astype(_BF16)
    y_ref[...] = ssd_out[0] + dskip_ref[...] * xs


def _mixer_front(h2d, seqlen, sinks, w_all, layer, conv_w, conv_b, dt_bias, a_log, d_skip_row, input_ln=None):
    n = h2d.shape[0]
    tm = SEQ_TILE
    nblk = n // tm
    tok_in = pl.BlockSpec((tm, D_MODEL), lambda i: (jnp.minimum(i, nblk - 1), 0))
    tok = lambda width: pl.BlockSpec((tm, width), lambda i: (jnp.maximum(i - 1, 0), 0))
    ln_specs = [_resident((1, D_MODEL)), _resident((1, D_MODEL))] if input_ln else []
    ln_out_specs = [tok_in] if input_ln else []
    ln_out_shapes = [jax.ShapeDtypeStruct((n, D_MODEL), _F32)] if input_ln else []
    ln_scratch = [pltpu.VMEM((D_MODEL // LANES, tm, LANES), _F32)] if input_ln else []
    return pl.pallas_call(
        functools.partial(_mixer_front_kernel, blocks_per_seq=seqlen // tm, input_ln=bool(input_ln)),
        grid=(nblk + 1,),
        in_specs=[pl.BlockSpec(memory_space=pltpu.SMEM), tok_in] + ln_specs + [
                  pl.BlockSpec((None, D_MODEL, FRONT_DIM), lambda i: (layer, 0, 0), pipeline_mode=pl.Buffered(1)),
                  _resident((SSD_CONV, CONV_DIM)), _resident((1, CONV_DIM)),
                  _resident((1, LANES)), _resident((1, LANES)),
                  _resident((1, SSD_D_INNER))],
        out_specs=[tok(SSD_D_INNER), tok(Q_DIM), pl.BlockSpec((tm, SSD_D_INNER), tok_in.index_map)] + ln_out_specs,
        out_shape=[jax.ShapeDtypeStruct((n, SSD_D_INNER), _F32),
                   jax.ShapeDtypeStruct((n, Q_DIM), _BF16),
                   jax.ShapeDtypeStruct((n, SSD_D_INNER), _F32)] + ln_out_shapes,
        scratch_shapes=[pltpu.VMEM((SSD_HEADS // 2, SSD_STATE, 2 * SSD_HEAD_DIM), _F32),
                        pltpu.VMEM((ATT_BLOCK, 2 * KV_DIM), _F32),
                        pltpu.VMEM(((SSD_CONV - 1) * SUBLANES, CONV_DIM), _F32),
                        pltpu.VMEM((tm, FRONT_DIM), _F32)] + ln_scratch,
        compiler_params=pltpu.CompilerParams(dimension_semantics=("arbitrary",),
                                             vmem_limit_bytes=VMEM_LIMIT),
        name="mixer_front",
    )(sinks, h2d, *(input_ln or ()), w_all, conv_w, conv_b, dt_bias, a_log, d_skip_row)


def _mixer_back_kernel(h_ref, y_ref, z_ref, att_ref, normw_ref, wg_ref, wa_ref, wb_ref, wm_ref, g_ref, b_ref, o_ref):
    h = h_ref[...]
    y_b = _dot(att_ref[...], wb_ref[...])
    gates = _sigmoid(_dot(h.astype(_BF16), wg_ref[...]))
    z = z_ref[...]
    y = y_ref[...] * (z * _sigmoid(z))
    gsz = SSD_D_INNER // SSD_GROUPS
    parts = []
    for g in range(SSD_GROUPS):
        yg = y[:, g * gsz:(g + 1) * gsz]
        ms = jnp.mean(yg * yg, axis=-1, keepdims=True)
        parts.append(yg * lax.rsqrt(ms + RMS_EPS))
    y = (jnp.concatenate(parts, axis=-1) * normw_ref[...]).astype(_BF16)
    y_a = _dot(y, wa_ref[...])
    merged = gates[:, :D_MODEL] * y_a + gates[:, D_MODEL:] * y_b
    mix = _dot(merged.astype(_BF16), wm_ref[...])
    o_ref[...] = _layer_norm(DEEPNORM_ALPHA * h + mix, g_ref[...], b_ref[...])


def _mixer_back(h2d, y2d, z2d, att2d, norm_w, w_gates, w_ssd_out, w_att_out, w_mix_out, ln_g, ln_b):
    n = h2d.shape[0]
    rows = lambda width: pl.BlockSpec((ROW_TILE, width), lambda i: (i, 0))
    return pl.pallas_call(
        _mixer_back_kernel,
        grid=(n // ROW_TILE,),
        in_specs=[rows(D_MODEL), rows(SSD_D_INNER), rows(SSD_D_INNER), rows(Q_DIM), _resident((1, SSD_D_INNER)),
                  _resident((D_MODEL, 2 * D_MODEL)), _resident((SSD_D_INNER, D_MODEL)),
                  _resident((Q_DIM, D_MODEL)), _resident((D_MODEL, D_MODEL)),
                  _resident((1, D_MODEL)), _resident((1, D_MODEL))],
        out_specs=rows(D_MODEL),
        out_shape=jax.ShapeDtypeStruct((n, D_MODEL), _F32),
        compiler_params=pltpu.CompilerParams(dimension_semantics=("arbitrary",),
                                             vmem_limit_bytes=VMEM_LIMIT),
        name="mixer_back",
    )(h2d, y2d, z2d, att2d, norm_w, w_gates, w_ssd_out, w_att_out, w_mix_out, ln_g, ln_b)


def _ffn_kernel(h_ref, wg_ref, wu_ref, wd_ref, g_ref, b_ref, o_ref, *scratch, restore_time_order):
    h = _load_sublane_major_to_time_major(h_ref, scratch[0]) if restore_time_order else h_ref[...]
    hb = h.astype(_BF16)
    gate = _dot(hb, wg_ref[...])
    up = _dot(hb, wu_ref[...])
    act = (gate * _sigmoid(gate) * up).astype(_BF16)
    ffn = _dot(act, wd_ref[...])
    o_ref[...] = _layer_norm(DEEPNORM_ALPHA * h + ffn, g_ref[...], b_ref[...])


def _ffn(h2d, w_gate, w_up, w_down, ln_g, ln_b, restore_time_order):
    n = h2d.shape[0]
    rows = pl.BlockSpec((ROW_TILE, D_MODEL), lambda i: (i, 0))
    return pl.pallas_call(
        functools.partial(_ffn_kernel, restore_time_order=restore_time_order),
        grid=(n // ROW_TILE,),
        in_specs=[rows, _resident((D_MODEL, FFN_HIDDEN)), _resident((D_MODEL, FFN_HIDDEN)),
                  _resident((FFN_HIDDEN, D_MODEL)), _resident((1, D_MODEL)), _resident((1, D_MODEL))],
        out_specs=rows,
        out_shape=jax.ShapeDtypeStruct((n, D_MODEL), _F32),
        scratch_shapes=[pltpu.VMEM((D_MODEL // LANES, ROW_TILE, LANES), _F32)] if restore_time_order else [],
        compiler_params=pltpu.CompilerParams(dimension_semantics=("arbitrary",),
                                             vmem_limit_bytes=VMEM_LIMIT),
        name="ffn",
    )(h2d, w_gate, w_up, w_down, ln_g, ln_b)


def kernel(x, ln_in_g, ln_in_b, w_in, conv_w, conv_b, dt_bias, a_log, d_skip, ssd_norm_w, att_sinks,
           w_ssd_out, w_att_out, w_mix_out, ln_mix_g, ln_mix_b, w_ffn_gate, w_ffn_up, w_ffn_down,
           ln_ffn_g, ln_ffn_b):
    bsz, seqlen, _ = x.shape
    n = bsz * seqlen
    row = lambda v: v.reshape(1, -1)
    pad_heads = lambda v: jnp.pad(v, (0, LANES - SSD_HEADS)).reshape(1, LANES)

    h = x.reshape(n, D_MODEL)
    w_in_b = w_in.astype(_BF16)
    for l in range(DEPTH):
        y, att, z, *normed = _mixer_front(
            h, seqlen, att_sinks[l], w_in_b, l, conv_w[l], row(conv_b[l]),
            pad_heads(dt_bias[l]), pad_heads(a_log[l]), row(jnp.repeat(d_skip[l], SSD_HEAD_DIM)),
            input_ln=(row(ln_in_g), row(ln_in_b)) if l == 0 else None)
        if normed:
            h = normed[0]
        h = _mixer_back(h, y, z, att, row(ssd_norm_w[l]), w_in_b[l, :, OFF_GATES:],
                        w_ssd_out[l].astype(_BF16), w_att_out[l].astype(_BF16),
                        w_mix_out[l].astype(_BF16), row(ln_mix_g[l]), row(ln_mix_b[l]))
        h = _ffn(h, w_ffn_gate[l].astype(_BF16), w_ffn_up[l].astype(_BF16),
                 w_ffn_down[l].astype(_BF16), row(ln_ffn_g[l]), row(ln_ffn_b[l]),
                 restore_time_order=(l == DEPTH - 1))
    return h.reshape(bsz, seqlen, D_MODEL)
```

```python
import functools

import jax
import jax.numpy as jnp
from jax import lax
from jax.experimental import pallas as pl
from jax.experimental.pallas import tpu as pltpu

D_MODEL = 1024
ATT_HEADS = 16
ATT_KV_HEADS = 2
ATT_HEAD_DIM = 64
ATT_BLOCK = 128
SSD_D_INNER = 2048
SSD_HEAD_DIM = 64
SSD_HEADS = 32
SSD_GROUPS = 4
SSD_HEADS_PER_GROUP = SSD_HEADS // SSD_GROUPS
SSD_STATE = 128
SSD_CONV = 4
SSD_CHUNK = 128
FFN_HIDDEN = 2816
DEPTH = 2
LN_EPS = 1e-5
RMS_EPS = 1e-5
DEEPNORM_ALPHA = (2 * DEPTH) ** 0.25
LOG2E = 1.4426950408889634

Q_DIM = ATT_HEADS * ATT_HEAD_DIM
KV_DIM = ATT_KV_HEADS * ATT_HEAD_DIM
BC_DIM = SSD_GROUPS * SSD_STATE
CONV_DIM = SSD_D_INNER + 2 * BC_DIM

OFF_Q = 0
OFF_KV = OFF_Q + Q_DIM
OFF_Z = OFF_KV + 2 * KV_DIM
OFF_XBC = OFF_Z + SSD_D_INNER
OFF_DT = OFF_XBC + CONV_DIM
LANES = 128
SUBLANES = 8
FRONT_DIM = OFF_DT + LANES
OFF_GATES = OFF_DT + SSD_HEADS

SEQ_TILE = 128
TIME_GROUPS = SEQ_TILE // SUBLANES
ROW_TILE = 512
VMEM_LIMIT = 56 * 1024 * 1024

_F32 = jnp.float32
_BF16 = jnp.bfloat16
_NT = (((1,), (1,)), ((), ()))


def _dot(a, b):
    return jnp.dot(a, b, preferred_element_type=_F32)


def _sigmoid(x):
    return 1.0 / (1.0 + jnp.exp2(x * (-LOG2E)))


def _block_time(idx):
    return (idx & (SUBLANES - 1)) * TIME_GROUPS + idx // SUBLANES


def _strided_row_groups(ref, chunk_ref, groups):
    chunks = chunk_ref.shape[0]
    for c in range(chunks):
        chunk_ref[c] = ref[:, c * LANES:(c + 1) * LANES]
    pieces = []
    for start, stride in groups:
        pieces.append(jnp.concatenate(
            [chunk_ref[c, pl.ds(start, SUBLANES, stride=stride), :] for c in range(chunks)], axis=1))
    return jnp.concatenate(pieces, axis=0)


def _load_time_major_to_sublane_major(ref, chunk_ref):
    groups = [(blk * SEQ_TILE + v, TIME_GROUPS)
              for blk in range(ref.shape[0] // SEQ_TILE) for v in range(TIME_GROUPS)]
    return _strided_row_groups(ref, chunk_ref, groups)


def _load_sublane_major_to_time_major(ref, chunk_ref):
    groups = [(blk * SEQ_TILE + (a % 2) * (SEQ_TILE // 2) + a // 2, SUBLANES)
              for blk in range(ref.shape[0] // SEQ_TILE) for a in range(TIME_GROUPS)]
    return _strided_row_groups(ref, chunk_ref, groups)


def _layer_norm(x, g, b):
    mu = jnp.mean(x, axis=-1, keepdims=True)
    xc = x - mu
    var = jnp.mean(xc * xc, axis=-1, keepdims=True)
    return xc * lax.rsqrt(var + LN_EPS) * g + b


def _resident(shape):
    nd = len(shape)
    return pl.BlockSpec(shape, lambda *_: (0,) * nd, pipeline_mode=pl.Buffered(1))


def _attention_phases(q, kv_prev, kv_cur, sinks_ref, first_block, out_list):
    blk = ATT_BLOCK
    half = ATT_HEAD_DIM
    band = jnp.concatenate([kv_prev, kv_cur], axis=0)
    k_tile, v_tile = band[:, :KV_DIM], band[:, KV_DIM:]
    swap = lambda t: jnp.concatenate([t[:, half:], t[:, :half]], axis=1)
    k_swap, v_swap = swap(k_tile), swap(v_tile)
    lane_kv = lax.broadcasted_iota(jnp.int32, (2 * blk, KV_DIM), 1)
    low_kv = lane_kv < half
    zero = jnp.zeros((2 * blk, KV_DIM), _F32)
    k_low = (jnp.where(low_kv, k_tile, zero).astype(_BF16), jnp.where(low_kv, k_swap, zero).astype(_BF16))
    k_high = (jnp.where(low_kv, zero, k_swap).astype(_BF16), jnp.where(low_kv, zero, k_tile).astype(_BF16))
    ones = jnp.ones((2 * blk, KV_DIM), _BF16)
    v_ext = (jnp.concatenate([jnp.where(low_kv, v_tile, v_swap).astype(_BF16), ones], axis=1),
             jnp.concatenate([jnp.where(low_kv, v_swap, v_tile).astype(_BF16), ones], axis=1))

    row = lax.broadcasted_iota(jnp.int32, (blk, 2 * blk), 0)
    col = lax.broadcasted_iota(jnp.int32, (blk, 2 * blk), 1)
    key_time = _block_time(col & (blk - 1)) + (col & blk)
    rel = _block_time(row) + blk - key_time
    first_key = jnp.where(first_block, blk, 0)
    valid = (rel >= 0) & (rel < blk) & (col >= first_key)
    rel_f = rel.astype(_F32)
    low_out = lax.broadcasted_iota(jnp.int32, (blk, 2 * half), 1) < half
    group = ATT_HEADS // ATT_KV_HEADS
    scores = []
    for h in range(ATT_HEADS):
        q_pair = q[:, (h // 2) * 2 * half:(h // 2 + 1) * 2 * half]
        keys = (k_low if h % 2 == 0 else k_high)[h // group]
        scores.append(lax.dot_general(q_pair, keys, _NT, preferred_element_type=_F32))
    yield
    probs, sink_terms = [], []
    for h in range(ATT_HEADS):
        slope = LOG2E * 2.0 ** (-8.0 * (h + 1) / ATT_HEADS)
        s = jnp.where(valid, scores[h] - slope * rel_f, -jnp.inf)
        sink = sinks_ref[h] * LOG2E
        m = jnp.maximum(jnp.max(s, axis=-1, keepdims=True), sink)
        probs.append(jnp.exp2(s - m).astype(_BF16))
        sink_terms.append(jnp.exp2(sink - m))
    yield
    outs = []
    for j in range(ATT_HEADS // 2):
        halves = []
        for h in (2 * j, 2 * j + 1):
            res = _dot(probs[h], v_ext[h // group])
            halves.append(res[:, :2 * half] / (res[:, 2 * half:] + sink_terms[h]))
        outs.append(jnp.where(low_out, halves[0], halves[1]))
    out_list.append(jnp.concatenate(outs, axis=-1))


def _ssd_phases(xs, b_mat, c_mat, dt, a_row, s_ref, out_list):
    t = SSD_CHUNK
    hp = SSD_HEAD_DIM
    d_a = dt * a_row
    d_a_hi = d_a.astype(_BF16)
    d_a_lo = (d_a - d_a_hi.astype(_F32)).astype(_BF16)
    r = lax.broadcasted_iota(jnp.int32, (t, t), 0)
    c = lax.broadcasted_iota(jnp.int32, (t, t), 1)
    causal = _block_time(r) >= _block_time(c)
    tri = jnp.where(causal, 1.0, 0.0).astype(_BF16)
    cum2 = (_dot(tri, d_a_hi) + _dot(tri, d_a_lo)) * LOG2E
    c_gs = [c_mat[:, g * SSD_STATE:(g + 1) * SSD_STATE] for g in range(SSD_GROUPS)]
    b_gs = [b_mat[:, g * SSD_STATE:(g + 1) * SSD_STATE] for g in range(SSD_GROUPS)]
    cbs = [lax.dot_general(c_gs[g].astype(_BF16), b_gs[g].astype(_BF16), _NT, preferred_element_type=_F32)
           for g in range(SSD_GROUPS)]
    yield
    cum2_t = cum2.T
    src2_t = cum2_t - (jnp.log(dt) * LOG2E).T
    w_state_t = jnp.exp2(cum2_t[:, t - 1:t] - src2_t)
    chunk_decay = jnp.exp2(cum2[t - 1:t, :])
    lhs, bw = [], []
    for g in range(SSD_GROUPS):
        b_g_t = b_gs[g].T
        for e in range(SSD_HEADS_PER_GROUP):
            h = g * SSD_HEADS_PER_GROUP + e
            dst2 = jnp.broadcast_to(cum2[:, h:h + 1], (t, t))
            seg2 = jnp.where(causal, dst2 - src2_t[h:h + 1, :], -jnp.inf)
            m_intra = jnp.exp2(seg2) * cbs[g]
            c_scaled = c_gs[g] * jnp.exp2(dst2)
            lhs.append(jnp.concatenate([m_intra, c_scaled], axis=1).astype(_BF16))
            bw.append((b_g_t * w_state_t[h:h + 1, :]).astype(_BF16))
    yield
    xs_b = xs.astype(_BF16)
    lane = lax.broadcasted_iota(jnp.int32, (t, 2 * hp), 1)
    low = lane < hp
    low_b = jnp.where(low, 1.0, 0.0).astype(_BF16)
    high_b = jnp.where(low, 0.0, 1.0).astype(_BF16)
    low_row = low[0:1, :]
    ys = []
    for j in range(SSD_HEADS // 2):
        x_pair = xs_b[:, 2 * j * hp:(2 * j + 2) * hp]
        state = s_ref[j]
        rhs = jnp.concatenate([x_pair, state.astype(_BF16)], axis=0)
        ys.append(jnp.where(low, _dot(lhs[2 * j], rhs), _dot(lhs[2 * j + 1], rhs)))
        x_split = jnp.concatenate([x_pair * low_b, x_pair * high_b], axis=0)
        decay = jnp.where(low_row, chunk_decay[:, 2 * j:2 * j + 1], chunk_decay[:, 2 * j + 1:2 * j + 2])
        s_ref[j] = state * decay + _dot(jnp.concatenate([bw[2 * j], bw[2 * j + 1]], axis=1), x_split)
    out_list.append(jnp.concatenate(ys, axis=-1))


def _mixer_front_kernel(*refs, blocks_per_seq, input_ln):
    refs = list(refs)
    sinks_ref, h_ref = refs[:2]
    del refs[:2]
    if input_ln:
        lng_ref, lnb_ref = refs[:2]
        del refs[:2]
    w_ref, convw_ref, convb_ref, dtb_ref, alog_ref, dskip_ref, y_ref, att_ref = refs[:8]
    del refs[:8]
    if input_ln:
        hout_ref = refs.pop(0)
    s_ref, kv_ref, tail_ref, proj_ref = refs[:4]
    i = pl.program_id(0)
    tm = SEQ_TILE
    tail_rows = (SSD_CONV - 1) * SUBLANES
    starts_sequence = lax.rem(i - 1, blocks_per_seq) == 0

    @pl.when(i == 0)
    def _():
        proj_ref[...] = jnp.zeros_like(proj_ref)

    @pl.when(jnp.logical_or(i == 0, starts_sequence))
    def _():
        s_ref[...] = jnp.zeros_like(s_ref)
        kv_ref[...] = jnp.zeros_like(kv_ref)
        tail_ref[...] = jnp.zeros_like(tail_ref)

    if input_ln:
        h = _layer_norm(_load_time_major_to_sublane_major(h_ref, refs[4]), lng_ref[...], lnb_ref[...])
        hout_ref[...] = h
    else:
        h = h_ref[...]
    hb = h.astype(_BF16)

    def project(lo, hi):
        proj_ref[:, lo:hi] = _dot(hb, w_ref[:, lo:hi])

    q = proj_ref[:, OFF_Q:OFF_KV].astype(_BF16)
    kv_lane = lax.broadcasted_iota(jnp.int32, (1, 2 * KV_DIM), 1)
    kv_cur = proj_ref[:, OFF_KV:OFF_Z] * jnp.where(kv_lane < KV_DIM, LOG2E * ATT_HEAD_DIM ** -0.5, 1.0)
    xbc_pre = proj_ref[:, OFF_XBC:OFF_DT]
    dt_raw = proj_ref[:, OFF_DT:FRONT_DIM]

    att_out, ssd_out = [], []
    att = _attention_phases(q, kv_ref[...], kv_cur, sinks_ref, starts_sequence, att_out)
    next(att)
    kv_ref[...] = kv_cur
    project(OFF_Q, OFF_Z)

    tail = xbc_pre[tm - tail_rows:, :]
    sublane = lax.broadcasted_iota(jnp.int32, (SUBLANES, CONV_DIM), 0)
    wrapped = []
    for j in range(SSD_CONV - 1):
        rows = slice(j * SUBLANES, (j + 1) * SUBLANES)
        mixed = jnp.where(sublane == SUBLANES - 1, tail_ref[rows, :], tail[rows, :])
        wrapped.append(pltpu.roll(mixed, 1, axis=0))
    ext = jnp.concatenate(wrapped + [xbc_pre], axis=0)
    tail_ref[...] = tail
    acc = convb_ref[...]
    for k in range(SSD_CONV):
        acc = acc + convw_ref[k:k + 1, :] * ext[k * SUBLANES:k * SUBLANES + tm, :]
    xbc = acc * _sigmoid(acc)
    xs = xbc[:, :SSD_D_INNER]
    b_mat = xbc[:, SSD_D_INNER:SSD_D_INNER + BC_DIM]
    c_mat = xbc[:, SSD_D_INNER + BC_DIM:]
    dt_in = dt_raw + dtb_ref[...]
    dt = jnp.maximum(dt_in, 0.0) + jnp.log1p(jnp.exp(-jnp.abs(dt_in)))
    a_row = -jnp.exp(alog_ref[...])
    ssd = _ssd_phases(xs, b_mat, c_mat, dt, a_row, s_ref, ssd_out)
    next(ssd)
    project(OFF_XBC, FRONT_DIM)

    next(att)
    next(ssd)
    for phases in (att, ssd):
        for _ in phases:
            pass
    att_ref[...] = att_out[0].astype(_BF16)
    y_ref[...] = ssd_out[0] + dskip_ref[...] * xs


def _mixer_front(h2d, seqlen, sinks, w_all, layer, conv_w, conv_b, dt_bias, a_log, d_skip_row, input_ln=None):
    n = h2d.shape[0]
    tm = SEQ_TILE
    nblk = n // tm
    tok_in = pl.BlockSpec((tm, D_MODEL), lambda i: (jnp.minimum(i, nblk - 1), 0))
    tok = lambda width: pl.BlockSpec((tm, width), lambda i: (jnp.maximum(i - 1, 0), 0))
    ln_specs = [_resident((1, D_MODEL)), _resident((1, D_MODEL))] if input_ln else []
    ln_out_specs = [tok_in] if input_ln else []
    ln_out_shapes = [jax.ShapeDtypeStruct((n, D_MODEL), _F32)] if input_ln else []
    ln_scratch = [pltpu.VMEM((D_MODEL // LANES, tm, LANES), _F32)] if input_ln else []
    return pl.pallas_call(
        functools.partial(_mixer_front_kernel, blocks_per_seq=seqlen // tm, input_ln=bool(input_ln)),
        grid=(nblk + 1,),
        in_specs=[pl.BlockSpec(memory_space=pltpu.SMEM), tok_in] + ln_specs + [
                  pl.BlockSpec((None, D_MODEL, FRONT_DIM), lambda i: (layer, 0, 0), pipeline_mode=pl.Buffered(1)),
                  _resident((SSD_CONV, CONV_DIM)), _resident((1, CONV_DIM)),
                  _resident((1, LANES)), _resident((1, LANES)),
                  _resident((1, SSD_D_INNER))],
        out_specs=[tok(SSD_D_INNER), tok(Q_DIM)] + ln_out_specs,
        out_shape=[jax.ShapeDtypeStruct((n, SSD_D_INNER), _F32),
                   jax.ShapeDtypeStruct((n, Q_DIM), _BF16)] + ln_out_shapes,
        scratch_shapes=[pltpu.VMEM((SSD_HEADS // 2, SSD_STATE, 2 * SSD_HEAD_DIM), _F32),
                        pltpu.VMEM((ATT_BLOCK, 2 * KV_DIM), _F32),
                        pltpu.VMEM(((SSD_CONV - 1) * SUBLANES, CONV_DIM), _F32),
                        pltpu.VMEM((tm, FRONT_DIM), _F32)] + ln_scratch,
        compiler_params=pltpu.CompilerParams(dimension_semantics=("arbitrary",),
                                             vmem_limit_bytes=VMEM_LIMIT),
        name="mixer_front",
    )(sinks, h2d, *(input_ln or ()), w_all, conv_w, conv_b, dt_bias, a_log, d_skip_row)


def _mixer_back_kernel(h_ref, y_ref, att_ref, normw_ref, wz_ref, wg_ref, wa_ref, wb_ref, wm_ref, g_ref, b_ref, o_ref):
    h = h_ref[...]
    hb = h.astype(_BF16)
    z = _dot(hb, wz_ref[...])
    y_b = _dot(att_ref[...], wb_ref[...])
    gates = _sigmoid(_dot(hb, wg_ref[...]))
    y = y_ref[...] * (z * _sigmoid(z))
    gsz = SSD_D_INNER // SSD_GROUPS
    parts = []
    for g in range(SSD_GROUPS):
        yg = y[:, g * gsz:(g + 1) * gsz]
        ms = jnp.mean(yg * yg, axis=-1, keepdims=True)
        parts.append(yg * lax.rsqrt(ms + RMS_EPS))
    y = (jnp.concatenate(parts, axis=-1) * normw_ref[...]).astype(_BF16)
    y_a = _dot(y, wa_ref[...])
    merged = gates[:, :D_MODEL] * y_a + gates[:, D_MODEL:] * y_b
    mix = _dot(merged.astype(_BF16), wm_ref[...])
    o_ref[...] = _layer_norm(DEEPNORM_ALPHA * h + mix, g_ref[...], b_ref[...])


def _mixer_back(h2d, y2d, att2d, norm_w, w_z, w_gates, w_ssd_out, w_att_out, w_mix_out, ln_g, ln_b):
    n = h2d.shape[0]
    rows = lambda width: pl.BlockSpec((ROW_TILE, width), lambda i: (i, 0))
    return pl.pallas_call(
        _mixer_back_kernel,
        grid=(n // ROW_TILE,),
        in_specs=[rows(D_MODEL), rows(SSD_D_INNER), rows(Q_DIM), _resident((1, SSD_D_INNER)),
                  _resident((D_MODEL, SSD_D_INNER)),
                  _resident((D_MODEL, 2 * D_MODEL)), _resident((SSD_D_INNER, D_MODEL)),
                  _resident((Q_DIM, D_MODEL)), _resident((D_MODEL, D_MODEL)),
                  _resident((1, D_MODEL)), _resident((1, D_MODEL))],
        out_specs=rows(D_MODEL),
        out_shape=jax.ShapeDtypeStruct((n, D_MODEL), _F32),
        compiler_params=pltpu.CompilerParams(dimension_semantics=("arbitrary",),
                                             vmem_limit_bytes=VMEM_LIMIT),
        name="mixer_back",
    )(h2d, y2d, att2d, norm_w, w_z, w_gates, w_ssd_out, w_att_out, w_mix_out, ln_g, ln_b)


def _ffn_kernel(h_ref, wg_ref, wu_ref, wd_ref, g_ref, b_ref, o_ref, *scratch, restore_time_order):
    h = _load_sublane_major_to_time_major(h_ref, scratch[0]) if restore_time_order else h_ref[...]
    hb = h.astype(_BF16)
    gate = _dot(hb, wg_ref[...])
    up = _dot(hb, wu_ref[...])
    act = (gate * _sigmoid(gate) * up).astype(_BF16)
    ffn = _dot(act, wd_ref[...])
    o_ref[...] = _layer_norm(DEEPNORM_ALPHA * h + ffn, g_ref[...], b_ref[...])


def _ffn(h2d, w_gate, w_up, w_down, ln_g, ln_b, restore_time_order):
    n = h2d.shape[0]
    rows = pl.BlockSpec((ROW_TILE, D_MODEL), lambda i: (i, 0))
    return pl.pallas_call(
        functools.partial(_ffn_kernel, restore_time_order=restore_time_order),
        grid=(n // ROW_TILE,),
        in_specs=[rows, _resident((D_MODEL, FFN_HIDDEN)), _resident((D_MODEL, FFN_HIDDEN)),
                  _resident((FFN_HIDDEN, D_MODEL)), _resident((1, D_MODEL)), _resident((1, D_MODEL))],
        out_specs=rows,
        out_shape=jax.ShapeDtypeStruct((n, D_MODEL), _F32),
        scratch_shapes=[pltpu.VMEM((D_MODEL // LANES, ROW_TILE, LANES), _F32)] if restore_time_order else [],
        compiler_params=pltpu.CompilerParams(dimension_semantics=("arbitrary",),
                                             vmem_limit_bytes=VMEM_LIMIT),
        name="ffn",
    )(h2d, w_gate, w_up, w_down, ln_g, ln_b)


def kernel(x, ln_in_g, ln_in_b, w_in, conv_w, conv_b, dt_bias, a_log, d_skip, ssd_norm_w, att_sinks,
           w_ssd_out, w_att_out, w_mix_out, ln_mix_g, ln_mix_b, w_ffn_gate, w_ffn_up, w_ffn_down,
           ln_ffn_g, ln_ffn_b):
    bsz, seqlen, _ = x.shape
    n = bsz * seqlen
    row = lambda v: v.reshape(1, -1)
    pad_heads = lambda v: jnp.pad(v, (0, LANES - SSD_HEADS)).reshape(1, LANES)

    h = x.reshape(n, D_MODEL)
    w_in_b = w_in.astype(_BF16)
    for l in range(DEPTH):
        y, att, *normed = _mixer_front(
            h, seqlen, att_sinks[l], w_in_b, l, conv_w[l], row(conv_b[l]),
            pad_heads(dt_bias[l]), pad_heads(a_log[l]), row(jnp.repeat(d_skip[l], SSD_HEAD_DIM)),
            input_ln=(row(ln_in_g), row(ln_in_b)) if l == 0 else None)
        if normed:
            h = normed[0]
        h = _mixer_back(h, y, att, row(ssd_norm_w[l]), w_in_b[l, :, OFF_Z:OFF_XBC], w_in_b[l, :, OFF_GATES:],
                        w_ssd_out[l].astype(_BF16), w_att_out[l].astype(_BF16),
                        w_mix_out[l].astype(_BF16), row(ln_mix_g[l]), row(ln_mix_b[l]))
        h = _ffn(h, w_ffn_gate[l].astype(_BF16), w_ffn_up[l].astype(_BF16),
                 w_ffn_down[l].astype(_BF16), row(ln_ffn_g[l]), row(ln_ffn_b[l]),
                 restore_time_order=(l == DEPTH - 1))
    return h.reshape(bsz, seqlen, D_MODEL)
```
